```python
import jax, jax.numpy as jnp
from jax import lax
import numpy as np

D_MODEL = 2048
BATCH = 1
SEQ = 8192
DEPTH = 1

MEM_LEN = 256
HEAD_DIM = 128
N_CONV_GROUPS = 4
CONV_W = N_CONV_GROUPS * HEAD_DIM
CONV_K = 31
N_FOX_HEADS = 8
FOX_W = N_FOX_HEADS * HEAD_DIM
N_MEM_HEADS = 4
MEM_W = N_MEM_HEADS * HEAD_DIM
MIX_W = CONV_W + FOX_W + MEM_W
Q_BLOCK = 128
EPS = 1e-6
FORGET_BIAS_INIT = 2.0

SPLITS = (
    CONV_W,
    2 * CONV_W,
    3 * CONV_W,
    3 * CONV_W + FOX_W,
    3 * CONV_W + 2 * FOX_W,
    3 * CONV_W + 3 * FOX_W,
    3 * CONV_W + 3 * FOX_W + N_FOX_HEADS,
    3 * CONV_W + 4 * FOX_W + N_FOX_HEADS,
    3 * CONV_W + 4 * FOX_W + N_FOX_HEADS + MEM_W,
)
D_IN = 3 * CONV_W + 4 * FOX_W + N_FOX_HEADS + 2 * MEM_W

kernel_name = "hybrid_conformer_fox_memory_layer"


def rmsnorm(x, g):
    xf = x.astype(jnp.float32)
    y = xf * lax.rsqrt(jnp.mean(xf * xf, axis=-1, keepdims=True) + EPS)
    return (y * g.astype(jnp.float32)).astype(x.dtype)


def layernorm(x, g, b):
    xf = x.astype(jnp.float32)
    mu = jnp.mean(xf, axis=-1, keepdims=True)
    var = jnp.mean(jnp.square(xf - mu), axis=-1, keepdims=True)
    y = (xf - mu) * lax.rsqrt(var + EPS)
    return (y * g.astype(jnp.float32) + b.astype(jnp.float32)).astype(x.dtype)


def causal_depthwise_conv(u, w, b):
    y = lax.conv_general_dilated(
        u, w[:, None, :], window_strides=(1,), padding=[(CONV_K - 1, 0)],
        dimension_numbers=('NWC', 'WIO', 'NWC'), feature_group_count=u.shape[-1])
    return y + b


def fox_attention(q, k, v, logf):
    B, S, H, Dh = q.shape
    nb = S // Q_BLOCK
    cT = jnp.cumsum(logf, axis=1).transpose(0, 2, 1)
    qb = q.reshape(B, nb, Q_BLOCK, H, Dh).transpose(1, 0, 2, 3, 4)
    cb = cT.reshape(B, H, nb, Q_BLOCK).transpose(2, 0, 1, 3)
    k_pos = jnp.arange(S)
    scale = Dh ** -0.5

    def block(args):
        i, qi, ci = args
        s = jnp.einsum('bqhd,bkhd->bhqk', qi, k).astype(jnp.float32) * scale
        bias = ci[..., :, None] - cT[..., None, :]
        q_pos = i * Q_BLOCK + jnp.arange(Q_BLOCK)
        mask = k_pos[None, :] <= q_pos[:, None]
        s = jnp.where(mask, s + bias, -jnp.inf)
        p = jax.nn.softmax(s, axis=-1).astype(v.dtype)
        return jnp.einsum('bhqk,bkhd->bqhd', p, v)

    out = lax.map(block, (jnp.arange(nb), qb, cb))
    return out.transpose(1, 0, 2, 3, 4).reshape(B, S, H * Dh)


def memory_attention(q, mk, mv):
    B, S, H, Dh = q.shape
    s = jnp.einsum('bshd,bmhd->bhsm', q, mk).astype(jnp.float32) * (Dh ** -0.5)
    p = jax.nn.softmax(s, axis=-1).astype(mv.dtype)
    return jnp.einsum('bhsm,bmhd->bshd', p, mv).reshape(B, S, H * Dh)


def setup_inputs(seed: int = 0) -> dict:
    key = jax.random.key(seed)
    ks = jax.random.split(key, 16)
    n = jax.random.normal
    f32 = jnp.float32
    return {
        "x": n(ks[0], (BATCH, SEQ, D_MODEL), f32),
        "mem": n(ks[1], (BATCH, MEM_LEN, D_MODEL), f32),
        "norm_g": 1.0 + 0.05 * n(ks[2], (DEPTH, D_MODEL), f32),
        "mem_norm_g": 1.0 + 0.05 * n(ks[3], (DEPTH, D_MODEL), f32),
        "w_in": n(ks[4], (DEPTH, D_MODEL, D_IN), f32) * D_MODEL ** -0.5,
        "b_f": FORGET_BIAS_INIT + 0.1 * n(ks[5], (DEPTH, N_FOX_HEADS), f32),
        "conv_w": n(ks[6], (DEPTH, CONV_K, CONV_W), f32) * CONV_K ** -0.5,
        "conv_b": 0.02 * n(ks[7], (DEPTH, CONV_W), f32),
        "conv_ln_g": 1.0 + 0.05 * n(ks[8], (DEPTH, CONV_W), f32),
        "conv_ln_b": 0.02 * n(ks[9], (DEPTH, CONV_W), f32),
        "w_conv_pw": n(ks[10], (DEPTH, CONV_W, CONV_W), f32) * CONV_W ** -0.5,
        "w_mem_kv": n(ks[11], (DEPTH, D_MODEL, 2 * MEM_W), f32) * D_MODEL ** -0.5,
        "w_out": n(ks[12], (DEPTH, MIX_W, D_MODEL), f32) * MIX_W ** -0.5,
        "final_g": 1.0 + 0.05 * n(ks[13], (D_MODEL,), f32),
    }


def reference(x, mem, norm_g, mem_norm_g, w_in, b_f, conv_w, conv_b, conv_ln_g,
              conv_ln_b, w_conv_pw, w_mem_kv, w_out, final_g):
    B, S, _ = x.shape
    for l in range(DEPTH):
        h = rmsnorm(x, norm_g[l])
        proj = h @ w_in[l]
        (cv_a, cv_b, cv_gate, fq, fk, fv, f_logit, fox_gate,
         mq, mem_gate) = jnp.split(proj, SPLITS, axis=-1)

        u = cv_a * jax.nn.sigmoid(cv_b)
        u = causal_depthwise_conv(u, conv_w[l], conv_b[l])
        u = jax.nn.silu(layernorm(u, conv_ln_g[l], conv_ln_b[l]))
        y_conv = (u @ w_conv_pw[l]) * jax.nn.silu(cv_gate)

        logf = jax.nn.log_sigmoid(f_logit.astype(jnp.float32) + b_f[l].astype(jnp.float32))
        shp = (B, S, N_FOX_HEADS, HEAD_DIM)
        y_fox = fox_attention(fq.reshape(shp), fk.reshape(shp), fv.reshape(shp), logf)
        y_fox = y_fox * jax.nn.silu(fox_gate)

        mkv = rmsnorm(mem, mem_norm_g[l]) @ w_mem_kv[l]
        mk, mv = jnp.split(mkv, 2, axis=-1)
        mshp = (B, mem.shape[1], N_MEM_HEADS, HEAD_DIM)
        y_mem = memory_attention(mq.reshape(B, S, N_MEM_HEADS, HEAD_DIM),
                                 mk.reshape(mshp), mv.reshape(mshp))
        y_mem = y_mem * jax.nn.silu(mem_gate)

        y = jnp.concatenate([y_conv, y_fox, y_mem], axis=-1)
        x = x + y @ w_out[l]
    return rmsnorm(x, final_g)
```

```python
import functools

import jax
import jax.numpy as jnp
from jax import lax
from jax.experimental import pallas as pl
from jax.experimental.pallas import tpu as pltpu

F32 = jnp.float32
BF16 = jnp.bfloat16

LANES = 128
HEAD_DIM = 128
N_FOX_HEADS = 8
N_MEM_HEADS = 4
CONV_W = 512
FOX_W = N_FOX_HEADS * HEAD_DIM
MEM_W = N_MEM_HEADS * HEAD_DIM
CONV_K = 31
CONV_HALO = 32
EPS = 1e-6
MASK_VALUE = -1e30

COL_CONV_A = 0
COL_CONV_B = CONV_W
COL_CONV_GATE = 2 * CONV_W
COL_Q = 3 * CONV_W
COL_K = COL_Q + FOX_W
COL_V = COL_K + FOX_W
COL_FOX_GATE = COL_V + FOX_W
COL_MQ = COL_FOX_GATE + FOX_W
COL_MEM_GATE = COL_MQ + MEM_W
N_MAIN = COL_MEM_GATE + MEM_W

VMEM_LIMIT_BYTES = 56 * 1024 * 1024


def _params(*sem):
    return pltpu.CompilerParams(dimension_semantics=sem, vmem_limit_bytes=VMEM_LIMIT_BYTES)


def _silu(x):
    return x * jax.nn.sigmoid(x)


def _in_proj_kernel(x_ref, g_ref, w_ref, wf_ref, cs_ref, o_ref, flog_ref, h_ref, *, row_chunk):
    n = pl.program_id(1)
    tm = x_ref.shape[0]

    @pl.when(n == 0)
    def _():
        def chunk(ci, carry):
            r = pl.multiple_of(ci * row_chunk, row_chunk)
            x = x_ref[pl.ds(r, row_chunk), :]
            ms = jnp.mean(x * x, axis=-1, keepdims=True)
            h = x * lax.rsqrt(ms + EPS) * g_ref[...]
            h_ref[pl.ds(r, row_chunk), :] = h.astype(BF16)
            return carry

        lax.fori_loop(0, tm // row_chunk, chunk, 0)
        flog_ref[...] = jnp.dot(h_ref[...], wf_ref[...], preferred_element_type=F32)

    acc = jnp.dot(h_ref[...], w_ref[...], preferred_element_type=F32)
    o_ref[...] = (acc * cs_ref[...]).astype(BF16)


def _in_proj(x, norm_g, w_main, w_f, col_scale, *, tm, tn):
    s, d = x.shape
    n_main = w_main.shape[1]
    return pl.pallas_call(
        functools.partial(_in_proj_kernel, row_chunk=128),
        grid=(s // tm, n_main // tn),
        in_specs=[
            pl.BlockSpec((tm, d), lambda i, n: (i, 0)),
            pl.BlockSpec((1, d), lambda i, n: (0, 0)),
            pl.BlockSpec((d, tn), lambda i, n: (0, n)),
            pl.BlockSpec((d, LANES), lambda i, n: (0, 0)),
            pl.BlockSpec((1, tn), lambda i, n: (0, n)),
        ],
        out_specs=[
            pl.BlockSpec((tm, tn), lambda i, n: (i, n)),
            pl.BlockSpec((tm, LANES), lambda i, n: (i, 0)),
        ],
        out_shape=[
            jax.ShapeDtypeStruct((s, n_main), BF16),
            jax.ShapeDtypeStruct((s, LANES), F32),
        ],
        scratch_shapes=[pltpu.VMEM((tm, d), BF16)],
        compiler_params=_params("arbitrary", "arbitrary"),
        name="in_proj",
    )(x, norm_g, w_main, w_f, col_scale)


def _decay_kernel(flog_ref, bf_ref, c_sh_ref, c_hs_ref):
    s = flog_ref.shape[0]
    logf = jax.nn.log_sigmoid(flog_ref[...] + bf_ref[...])
    c = logf.T
    pos = lax.broadcasted_iota(jnp.int32, c.shape, 1)
    shift = 1
    while shift < s:
        c = c + jnp.where(pos >= shift, pltpu.roll(c, shift, 1), 0.0)
        shift *= 2
    c_hs_ref[...] = c[:N_FOX_HEADS, :]
    c_sh_ref[...] = c.T


def _decay(flog, b_f_row):
    s = flog.shape[0]
    return pl.pallas_call(
        _decay_kernel,
        out_shape=[
            jax.ShapeDtypeStruct((s, LANES), F32),
            jax.ShapeDtypeStruct((N_FOX_HEADS, s), F32),
        ],
        compiler_params=pltpu.CompilerParams(vmem_limit_bytes=VMEM_LIMIT_BYTES),
        name="decay_cumsum",
    )(flog, b_f_row)


def _conv_kernel(a_ref, b_ref, ap_ref, bp_ref, gate_ref, cw_ref, cb_ref, lg_ref, lb_ref, pw_ref,
                 o_ref, u_ref, *, row_chunk):
    i = pl.program_id(0)
    tm = a_ref.shape[0]

    prev = ap_ref[...].astype(F32) * jax.nn.sigmoid(bp_ref[...].astype(F32))
    u_ref[0:CONV_HALO, :] = jnp.where(i > 0, prev, 0.0)
    u_ref[CONV_HALO:CONV_HALO + tm, :] = a_ref[...].astype(F32) * jax.nn.sigmoid(b_ref[...].astype(F32))

    first_tap = CONV_HALO - (CONV_K - 1)

    def chunk(ci, carry):
        r = pl.multiple_of(ci * row_chunk, row_chunk)
        win = u_ref[pl.ds(r, row_chunk + CONV_HALO), :]
        acc = jnp.zeros((row_chunk, CONV_W), F32) + cb_ref[...]
        for k in range(CONV_K):
            acc = acc + cw_ref[k:k + 1, :] * win[first_tap + k:first_tap + k + row_chunk, :]
        mu = jnp.mean(acc, axis=-1, keepdims=True)
        cen = acc - mu
        var = jnp.mean(cen * cen, axis=-1, keepdims=True)
        y = cen * lax.rsqrt(var + EPS) * lg_ref[...] + lb_ref[...]
        y = _silu(y).astype(BF16)
        pw = jnp.dot(y, pw_ref[...], preferred_element_type=F32)
        gate = gate_ref[pl.ds(r, row_chunk), :].astype(F32)
        o_ref[pl.ds(r, row_chunk), :] = (pw * _silu(gate)).astype(BF16)
        return carry

    lax.fori_loop(0, tm // row_chunk, chunk, 0)


def _conv_branch(proj, conv_w, conv_b, ln_g, ln_b, w_pw, *, tm):
    s = proj.shape[0]
    halo_blocks = tm // CONV_HALO
    col = lambda c: c // CONV_W

    def prev_map(c):
        return lambda i: (jnp.maximum(i * halo_blocks - 1, 0), col(c))

    full = lambda shape: pl.BlockSpec(shape, lambda i: (0, 0))
    return pl.pallas_call(
        functools.partial(_conv_kernel, row_chunk=64),
        grid=(s // tm,),
        in_specs=[
            pl.BlockSpec((tm, CONV_W), lambda i: (i, col(COL_CONV_A))),
            pl.BlockSpec((tm, CONV_W), lambda i: (i, col(COL_CONV_B))),
            pl.BlockSpec((CONV_HALO, CONV_W), prev_map(COL_CONV_A)),
            pl.BlockSpec((CONV_HALO, CONV_W), prev_map(COL_CONV_B)),
            pl.BlockSpec((tm, CONV_W), lambda i: (i, col(COL_CONV_GATE))),
            full((CONV_K, CONV_W)),
            full((1, CONV_W)),
            full((1, CONV_W)),
            full((1, CONV_W)),
            full((CONV_W, CONV_W)),
        ],
        out_specs=pl.BlockSpec((tm, CONV_W), lambda i: (i, 0)),
        out_shape=jax.ShapeDtypeStruct((s, CONV_W), BF16),
        scratch_shapes=[pltpu.VMEM((CONV_HALO + tm, CONV_W), F32)],
        compiler_params=_params("arbitrary"),
        name="conv_branch",
    )(proj, proj, proj, proj, proj, conv_w, conv_b, ln_g, ln_b, w_pw)


def _fox_kernel(q_ref, k_ref, v_ref, ccol_ref, crow_ref, gate_ref, o_ref, m_ref, l_ref, acc_ref, *, tk):
    h = pl.program_id(0)
    i = pl.program_id(1)
    tq = q_ref.shape[0]

    q = q_ref[...]
    lane = lax.broadcasted_iota(jnp.int32, (tq, LANES), 1)
    cq = jnp.sum(jnp.where(lane == h, ccol_ref[...], 0.0), axis=1, keepdims=True)

    m_ref[...] = jnp.full(m_ref.shape, MASK_VALUE, F32)
    l_ref[...] = jnp.zeros(l_ref.shape, F32)
    acc_ref[...] = jnp.zeros(acc_ref.shape, F32)

    def step(j, masked):
        start = pl.multiple_of(j * tk, tk)
        k = k_ref[pl.ds(start, tk), :]
        v = v_ref[pl.ds(start, tk), :]
        ck = crow_ref[pl.ds(h, 1), pl.ds(start, tk)]
        s = lax.dot_general(q, k, (((1,), (1,)), ((), ())), preferred_element_type=F32)
        s = s + (cq - ck)
        if masked:
            row = lax.broadcasted_iota(jnp.int32, s.shape, 0)
            colp = lax.broadcasted_iota(jnp.int32, s.shape, 1)
            s = jnp.where(colp <= row, s, MASK_VALUE)
        m_old = m_ref[...]
        m_new = jnp.maximum(m_old, jnp.max(s, axis=1, keepdims=True))
        alpha = jnp.exp(m_old - m_new)
        p = jnp.exp(s - m_new)
        l_ref[...] = alpha * l_ref[...] + jnp.sum(p, axis=1, keepdims=True)
        acc_ref[...] = alpha * acc_ref[...] + jnp.dot(p.astype(BF16), v, preferred_element_type=F32)
        m_ref[...] = m_new

    step(i, True)

    def body(j, carry):
        step(j, False)
        return carry

    lax.fori_loop(0, i, body, 0)

    gate = gate_ref[...].astype(F32)
    o_ref[...] = (acc_ref[...] / l_ref[...] * _silu(gate)).astype(BF16)


def _fox_attention(proj, c_sh, c_hs, *, tq):
    s = proj.shape[0]
    hcol = lambda c: c // HEAD_DIM
    return pl.pallas_call(
        functools.partial(_fox_kernel, tk=tq),
        grid=(N_FOX_HEADS, s // tq),
        in_specs=[
            pl.BlockSpec((tq, HEAD_DIM), lambda h, i: (i, hcol(COL_Q) + h)),
            pl.BlockSpec((s, HEAD_DIM), lambda h, i: (0, hcol(COL_K) + h)),
            pl.BlockSpec((s, HEAD_DIM), lambda h, i: (0, hcol(COL_V) + h)),
            pl.BlockSpec((tq, LANES), lambda h, i: (i, 0)),
            pl.BlockSpec((N_FOX_HEADS, s), lambda h, i: (0, 0)),
            pl.BlockSpec((tq, HEAD_DIM), lambda h, i: (i, hcol(COL_FOX_GATE) + h)),
        ],
        out_specs=pl.BlockSpec((tq, HEAD_DIM), lambda h, i: (i, h)),
        out_shape=jax.ShapeDtypeStruct((s, FOX_W), BF16),
        scratch_shapes=[
            pltpu.VMEM((tq, 1), F32),
            pltpu.VMEM((tq, 1), F32),
            pltpu.VMEM((tq, HEAD_DIM), F32),
        ],
        compiler_params=_params("arbitrary", "arbitrary"),
        name="fox_attention",
    )(proj, proj, proj, c_sh, c_hs, proj)


def _mem_kernel(mem_ref, mg_ref, wkv_ref, q_ref, gate_ref, o_ref, kv_ref):
    i = pl.program_id(0)

    @pl.when(i == 0)
    def _():
        mem = mem_ref[...]
        ms = jnp.mean(mem * mem, axis=-1, keepdims=True)
        hm = (mem * lax.rsqrt(ms + EPS) * mg_ref[...]).astype(BF16)
        kv_ref[...] = jnp.dot(hm, wkv_ref[...], preferred_element_type=F32).astype(BF16)

    scale = HEAD_DIM ** -0.5
    for hh in range(N_MEM_HEADS):
        lo = hh * HEAD_DIM
        q = q_ref[:, lo:lo + HEAD_DIM]
        k = kv_ref[:, lo:lo + HEAD_DIM]
        v = kv_ref[:, MEM_W + lo:MEM_W + lo + HEAD_DIM]
        s = lax.dot_general(q, k, (((1,), (1,)), ((), ())), preferred_element_type=F32) * scale
        m = jnp.max(s, axis=1, keepdims=True)
        p = jnp.exp(s - m)
        l = jnp.sum(p, axis=1, keepdims=True)
        pv = jnp.dot(p.astype(BF16), v, preferred_element_type=F32)
        gate = gate_ref[:, lo:lo + HEAD_DIM].astype(F32)
        o_ref[:, lo:lo + HEAD_DIM] = (pv / l * _silu(gate)).astype(BF16)


def _mem_attention(proj, mem, mem_norm_g, w_kv, *, tm):
    s = proj.shape[0]
    m_len, d = mem.shape
    col = lambda c: c // MEM_W
    full = lambda shape: pl.BlockSpec(shape, lambda i: (0, 0))
    return pl.pallas_call(
        _mem_kernel,
        grid=(s // tm,),
        in_specs=[
            full((m_len, d)),
            full((1, d)),
            full((d, 2 * MEM_W)),
            pl.BlockSpec((tm, MEM_W), lambda i: (i, col(COL_MQ))),
            pl.BlockSpec((tm, MEM_W), lambda i: (i, col(COL_MEM_GATE))),
        ],
        out_specs=pl.BlockSpec((tm, MEM_W), lambda i: (i, 0)),
        out_shape=jax.ShapeDtypeStruct((s, MEM_W), BF16),
        scratch_shapes=[pltpu.VMEM((m_len, 2 * MEM_W), BF16)],
        compiler_params=_params("arbitrary"),
        name="mem_attention",
    )(mem, mem_norm_g, w_kv, proj, proj)


def _out_kernel(x_ref, yc_ref, yf_ref, ym_ref, w_ref, g_ref, o_ref):
    acc = jnp.dot(yc_ref[...], w_ref[0:CONV_W, :], preferred_element_type=F32)
    acc = acc + jnp.dot(yf_ref[...], w_ref[CONV_W:CONV_W + FOX_W, :], preferred_element_type=F32)
    acc = acc + jnp.dot(ym_ref[...], w_ref[CONV_W + FOX_W:, :], preferred_element_type=F32)
    r = x_ref[...] + acc
    ms = jnp.mean(r * r, axis=-1, keepdims=True)
    o_ref[...] = r * lax.rsqrt(ms + EPS) * g_ref[...]


def _out_proj(x, y_conv, y_fox, y_mem, w_out, final_g, *, tm):
    s, d = x.shape
    row = lambda w: pl.BlockSpec((tm, w), lambda i: (i, 0))
    return pl.pallas_call(
        _out_kernel,
        grid=(s // tm,),
        in_specs=[
            row(d), row(CONV_W), row(FOX_W), row(MEM_W),
            pl.BlockSpec(w_out.shape, lambda i: (0, 0)),
            pl.BlockSpec((1, d), lambda i: (0, 0)),
        ],
        out_specs=row(d),
        out_shape=jax.ShapeDtypeStruct((s, d), F32),
        compiler_params=_params("arbitrary"),
        name="out_proj",
    )(x, y_conv, y_fox, y_mem, w_out, final_g)


def _layer(x, mem, norm_g, mem_norm_g, w_in, b_f, conv_w, conv_b, ln_g, ln_b, w_pw, w_kv, w_out):
    d = x.shape[-1]
    n_in_front = 3 * CONV_W + 3 * FOX_W
    w_main = jnp.concatenate(
        [w_in[:, :n_in_front], w_in[:, n_in_front + N_FOX_HEADS:]], axis=1).astype(BF16)
    w_f = jnp.pad(w_in[:, n_in_front:n_in_front + N_FOX_HEADS],
                  ((0, 0), (0, LANES - N_FOX_HEADS))).astype(BF16)
    col_scale = jnp.ones((1, N_MAIN), F32).at[:, COL_Q:COL_K].set(HEAD_DIM ** -0.5)
    b_f_row = jnp.pad(b_f, (0, LANES - N_FOX_HEADS)).reshape(1, LANES)

    proj, flog = _in_proj(x, norm_g.reshape(1, d), w_main, w_f, col_scale, tm=1024, tn=1664)
    c_sh, c_hs = _decay(flog, b_f_row)
    y_conv = _conv_branch(proj, conv_w, conv_b.reshape(1, -1), ln_g.reshape(1, -1), ln_b.reshape(1, -1),
                          w_pw.astype(BF16), tm=512)
    y_fox = _fox_attention(proj, c_sh, c_hs, tq=256)
    y_mem = _mem_attention(proj, mem, mem_norm_g.reshape(1, d), w_kv.astype(BF16), tm=1024)
    return y_conv, y_fox, y_mem


def kernel(x, mem, norm_g, mem_norm_g, w_in, b_f, conv_w, conv_b, conv_ln_g, conv_ln_b, w_conv_pw,
           w_mem_kv, w_out, final_g):
    b, s, d = x.shape
    assert w_in.shape[0] == 1, "a single trunk layer is supported"
    outs = []
    for bi in range(b):
        y_conv, y_fox, y_mem = _layer(
            x[bi], mem[bi], norm_g[0], mem_norm_g[0], w_in[0], b_f[0], conv_w[0], conv_b[0],
            conv_ln_g[0], conv_ln_b[0], w_conv_pw[0], w_mem_kv[0], w_out[0])
        outs.append(_out_proj(x[bi], y_conv, y_fox, y_mem, w_out[0].astype(BF16),
                              final_g.reshape(1, d), tm=512))
    return jnp.stack(outs, axis=0)
```

```python
import functools

import jax
import jax.numpy as jnp
from jax import lax
from jax.experimental import pallas as pl
from jax.experimental.pallas import tpu as pltpu

F32 = jnp.float32
BF16 = jnp.bfloat16

LANES = 128
HEAD_DIM = 128
N_FOX_HEADS = 8
N_MEM_HEADS = 4
CONV_W = 512
FOX_W = N_FOX_HEADS * HEAD_DIM
MEM_W = N_MEM_HEADS * HEAD_DIM
CONV_K = 31
CONV_HALO = 32
EPS = 1e-6
MASK_VALUE = -1e30
LOG2E = 1.4426950408889634

COL_CONV_A = 0
COL_CONV_B = CONV_W
COL_CONV_GATE = 2 * CONV_W
COL_Q = 3 * CONV_W
COL_K = COL_Q + FOX_W
COL_V = COL_K + FOX_W
COL_FOX_GATE = COL_V + FOX_W
COL_MQ = COL_FOX_GATE + FOX_W
COL_MEM_GATE = COL_MQ + MEM_W
N_MAIN = COL_MEM_GATE + MEM_W

VMEM_LIMIT_BYTES = 56 * 1024 * 1024


def _params(*sem):
    return pltpu.CompilerParams(dimension_semantics=sem, vmem_limit_bytes=VMEM_LIMIT_BYTES)


def _silu(x):
    return x * jax.nn.sigmoid(x)


def _in_proj_kernel(x_ref, g_ref, w_ref, wf_ref, cs_ref, o_ref, flog_ref, h_ref, *, row_chunk):
    n = pl.program_id(1)
    tm = x_ref.shape[0]

    @pl.when(n == 0)
    def _():
        def chunk(ci, carry):
            r = pl.multiple_of(ci * row_chunk, row_chunk)
            x = x_ref[pl.ds(r, row_chunk), :]
            ms = jnp.mean(x * x, axis=-1, keepdims=True)
            h = x * lax.rsqrt(ms + EPS) * g_ref[...]
            h_ref[pl.ds(r, row_chunk), :] = h.astype(BF16)
            return carry

        lax.fori_loop(0, tm // row_chunk, chunk, 0)
        flog_ref[...] = jnp.dot(h_ref[...], wf_ref[...], preferred_element_type=F32)

    acc = jnp.dot(h_ref[...], w_ref[...], preferred_element_type=F32)
    o_ref[...] = (acc * cs_ref[...]).astype(BF16)


def _in_proj(x, norm_g, w_main, w_f, col_scale, *, tm, tn):
    s, d = x.shape
    n_main = w_main.shape[1]
    return pl.pallas_call(
        functools.partial(_in_proj_kernel, row_chunk=128),
        grid=(s // tm, n_main // tn),
        in_specs=[
            pl.BlockSpec((tm, d), lambda i, n: (i, 0)),
            pl.BlockSpec((1, d), lambda i, n: (0, 0)),
            pl.BlockSpec((d, tn), lambda i, n: (0, n)),
            pl.BlockSpec((d, LANES), lambda i, n: (0, 0)),
            pl.BlockSpec((1, tn), lambda i, n: (0, n)),
        ],
        out_specs=[
            pl.BlockSpec((tm, tn), lambda i, n: (i, n)),
            pl.BlockSpec((tm, LANES), lambda i, n: (i, 0)),
        ],
        out_shape=[
            jax.ShapeDtypeStruct((s, n_main), BF16),
            jax.ShapeDtypeStruct((s, LANES), F32),
        ],
        scratch_shapes=[pltpu.VMEM((tm, d), BF16)],
        compiler_params=_params("arbitrary", "arbitrary"),
        name="in_proj",
    )(x, norm_g, w_main, w_f, col_scale)


def _decay_kernel(flog_ref, bf_ref, c_hs_ref):
    s = flog_ref.shape[0]
    logf = jax.nn.log_sigmoid(flog_ref[...] + bf_ref[...])
    c = logf.T[:N_FOX_HEADS, :]
    pos = lax.broadcasted_iota(jnp.int32, c.shape, 1)
    shift = 1
    while shift < s:
        c = c + jnp.where(pos >= shift, pltpu.roll(c, shift, 1), 0.0)
        shift *= 2
    c_hs_ref[...] = c * LOG2E


def _decay(flog, b_f_row):
    s = flog.shape[0]
    return pl.pallas_call(
        _decay_kernel,
        out_shape=jax.ShapeDtypeStruct((N_FOX_HEADS, s), F32),
        compiler_params=pltpu.CompilerParams(vmem_limit_bytes=VMEM_LIMIT_BYTES),
        name="decay_cumsum",
    )(flog, b_f_row)


def _conv_kernel(a_ref, b_ref, ap_ref, bp_ref, gate_ref, cw_ref, cb_ref, lg_ref, lb_ref, pw_ref,
                 o_ref, u_ref, *, row_chunk):
    i = pl.program_id(0)
    tm = a_ref.shape[0]

    prev = ap_ref[...].astype(F32) * jax.nn.sigmoid(bp_ref[...].astype(F32))
    u_ref[0:CONV_HALO, :] = jnp.where(i > 0, prev, 0.0)
    u_ref[CONV_HALO:CONV_HALO + tm, :] = a_ref[...].astype(F32) * jax.nn.sigmoid(b_ref[...].astype(F32))

    first_tap = CONV_HALO - (CONV_K - 1)

    def chunk(ci, carry):
        r = pl.multiple_of(ci * row_chunk, row_chunk)
        win = u_ref[pl.ds(r, row_chunk + CONV_HALO), :]
        acc = jnp.zeros((row_chunk, CONV_W), F32) + cb_ref[...]
        for k in range(CONV_K):
            acc = acc + cw_ref[k:k + 1, :] * win[first_tap + k:first_tap + k + row_chunk, :]
        mu = jnp.mean(acc, axis=-1, keepdims=True)
        cen = acc - mu
        var = jnp.mean(cen * cen, axis=-1, keepdims=True)
        y = cen * lax.rsqrt(var + EPS) * lg_ref[...] + lb_ref[...]
        y = _silu(y).astype(BF16)
        pw = jnp.dot(y, pw_ref[...], preferred_element_type=F32)
        gate = gate_ref[pl.ds(r, row_chunk), :].astype(F32)
        o_ref[pl.ds(r, row_chunk), :] = (pw * _silu(gate)).astype(BF16)
        return carry

    lax.fori_loop(0, tm // row_chunk, chunk, 0)


def _conv_branch(proj, conv_w, conv_b, ln_g, ln_b, w_pw, *, tm):
    s = proj.shape[0]
    halo_blocks = tm // CONV_HALO
    col = lambda c: c // CONV_W

    def prev_map(c):
        return lambda i: (jnp.maximum(i * halo_blocks - 1, 0), col(c))

    full = lambda shape: pl.BlockSpec(shape, lambda i: (0, 0))
    return pl.pallas_call(
        functools.partial(_conv_kernel, row_chunk=64),
        grid=(s // tm,),
        in_specs=[
            pl.BlockSpec((tm, CONV_W), lambda i: (i, col(COL_CONV_A))),
            pl.BlockSpec((tm, CONV_W), lambda i: (i, col(COL_CONV_B))),
            pl.BlockSpec((CONV_HALO, CONV_W), prev_map(COL_CONV_A)),
            pl.BlockSpec((CONV_HALO, CONV_W), prev_map(COL_CONV_B)),
            pl.BlockSpec((tm, CONV_W), lambda i: (i, col(COL_CONV_GATE))),
            full((CONV_K, CONV_W)),
            full((1, CONV_W)),
            full((1, CONV_W)),
            full((1, CONV_W)),
            full((CONV_W, CONV_W)),
        ],
        out_specs=pl.BlockSpec((tm, CONV_W), lambda i: (i, 0)),
        out_shape=jax.ShapeDtypeStruct((s, CONV_W), BF16),
        scratch_shapes=[pltpu.VMEM((CONV_HALO + tm, CONV_W), F32)],
        compiler_params=_params("arbitrary"),
        name="conv_branch",
    )(proj, proj, proj, proj, proj, conv_w, conv_b, ln_g, ln_b, w_pw)


def _fox_kernel(q_ref, k_ref, v_ref, crow_ref, gate_ref, o_ref, m_ref, l_ref, acc_ref, *, heads):
    hp = pl.program_id(0)
    i = pl.program_id(1)
    tq = q_ref.shape[0]
    tk = tq
    lane_reps = tk // LANES

    q_start = pl.multiple_of(i * tq, tq)
    cbase = [crow_ref[hp * heads + g, :, pl.ds(q_start, LANES)][:, 0:1] for g in range(heads)]

    m_ref[...] = jnp.full(m_ref.shape, MASK_VALUE, F32)
    l_ref[...] = jnp.zeros(l_ref.shape, F32)
    acc_ref[...] = jnp.zeros(acc_ref.shape, F32)

    def step(j, masked):
        start = pl.multiple_of(j * tk, tk)
        for g in range(heads):
            lo = g * HEAD_DIM
            q = q_ref[:, lo:lo + HEAD_DIM]
            k = k_ref[pl.ds(start, tk), lo:lo + HEAD_DIM]
            v = v_ref[pl.ds(start, tk), lo:lo + HEAD_DIM]
            ck = crow_ref[hp * heads + g, :, pl.ds(start, tk)] - cbase[g]
            s = lax.dot_general(q, k, (((1,), (1,)), ((), ())), preferred_element_type=F32)
            s = s - ck
            if masked:
                row = lax.broadcasted_iota(jnp.int32, s.shape, 0)
                colp = lax.broadcasted_iota(jnp.int32, s.shape, 1)
                s = jnp.where(colp <= row, s, MASK_VALUE)
            m_old = m_ref[g]
            m_new = jnp.maximum(m_old, jnp.max(s, axis=1, keepdims=True))
            alpha = jnp.exp2(m_old - m_new)
            p = jnp.exp2(s - pltpu.repeat(m_new, lane_reps, axis=1))
            l_ref[g] = alpha * l_ref[g] + jnp.sum(p, axis=1, keepdims=True)
            acc_ref[g] = alpha * acc_ref[g] + jnp.dot(p.astype(BF16), v, preferred_element_type=F32)
            m_ref[g] = m_new

    step(i, True)

    def body(jj, carry):
        step(i - 1 - jj, False)
        return carry

    lax.fori_loop(0, i, body, 0)

    for g in range(heads):
        lo = g * HEAD_DIM
        gate = gate_ref[:, lo:lo + HEAD_DIM].astype(F32)
        o_ref[:, lo:lo + HEAD_DIM] = (acc_ref[g] / l_ref[g] * _silu(gate)).astype(BF16)


def _fox_attention(proj, c_hs, *, tq, heads):
    s = proj.shape[0]
    w = heads * HEAD_DIM
    gcol = lambda c: c // w
    return pl.pallas_call(
        functools.partial(_fox_kernel, heads=heads),
        grid=(N_FOX_HEADS // heads, s // tq),
        in_specs=[
            pl.BlockSpec((tq, w), lambda h, i: (i, gcol(COL_Q) + h)),
            pl.BlockSpec((s, w), lambda h, i: (0, gcol(COL_K) + h)),
            pl.BlockSpec((s, w), lambda h, i: (0, gcol(COL_V) + h)),
            pl.BlockSpec((N_FOX_HEADS, 1, s), lambda h, i: (0, 0, 0)),
            pl.BlockSpec((tq, w), lambda h, i: (i, gcol(COL_FOX_GATE) + h)),
        ],
        out_specs=pl.BlockSpec((tq, w), lambda h, i: (i, h)),
        out_shape=jax.ShapeDtypeStruct((s, FOX_W), BF16),
        scratch_shapes=[
            pltpu.VMEM((heads, tq, LANES), F32),
            pltpu.VMEM((heads, tq, LANES), F32),
            pltpu.VMEM((heads, tq, HEAD_DIM), F32),
        ],
        compiler_params=_params("arbitrary", "arbitrary"),
        name="fox_attention",
    )(proj, proj, proj, c_hs, proj)


def _mem_kernel(mem_ref, mg_ref, wkv_ref, q_ref, gate_ref, o_ref, kv_ref):
    i = pl.program_id(0)

    @pl.when(i == 0)
    def _():
        mem = mem_ref[...]
        ms = jnp.mean(mem * mem, axis=-1, keepdims=True)
        hm = (mem * lax.rsqrt(ms + EPS) * mg_ref[...]).astype(BF16)
        kv_ref[...] = jnp.dot(hm, wkv_ref[...], preferred_element_type=F32).astype(BF16)

    scale = HEAD_DIM ** -0.5
    for hh in range(N_MEM_HEADS):
        lo = hh * HEAD_DIM
        q = q_ref[:, lo:lo + HEAD_DIM]
        k = kv_ref[:, lo:lo + HEAD_DIM]
        v = kv_ref[:, MEM_W + lo:MEM_W + lo + HEAD_DIM]
        s = lax.dot_general(q, k, (((1,), (1,)), ((), ())), preferred_element_type=F32) * scale
        m = jnp.max(s, axis=1, keepdims=True)
        p = jnp.exp(s - m)
        l = jnp.sum(p, axis=1, keepdims=True)
        pv = jnp.dot(p.astype(BF16), v, preferred_element_type=F32)
        gate = gate_ref[:, lo:lo + HEAD_DIM].astype(F32)
        o_ref[:, lo:lo + HEAD_DIM] = (pv / l * _silu(gate)).astype(BF16)


def _mem_attention(proj, mem, mem_norm_g, w_kv, *, tm):
    s = proj.shape[0]
    m_len, d = mem.shape
    col = lambda c: c // MEM_W
    full = lambda shape: pl.BlockSpec(shape, lambda i: (0, 0))
    return pl.pallas_call(
        _mem_kernel,
        grid=(s // tm,),
        in_specs=[
            full((m_len, d)),
            full((1, d)),
            full((d, 2 * MEM_W)),
            pl.BlockSpec((tm, MEM_W), lambda i: (i, col(COL_MQ))),
            pl.BlockSpec((tm, MEM_W), lambda i: (i, col(COL_MEM_GATE))),
        ],
        out_specs=pl.BlockSpec((tm, MEM_W), lambda i: (i, 0)),
        out_shape=jax.ShapeDtypeStruct((s, MEM_W), BF16),
        scratch_shapes=[pltpu.VMEM((m_len, 2 * MEM_W), BF16)],
        compiler_params=_params("arbitrary"),
        name="mem_attention",
    )(mem, mem_norm_g, w_kv, proj, proj)


def _out_kernel(x_ref, yc_ref, yf_ref, ym_ref, w_ref, g_ref, o_ref):
    acc = jnp.dot(yc_ref[...], w_ref[0:CONV_W, :], preferred_element_type=F32)
    acc = acc + jnp.dot(yf_ref[...], w_ref[CONV_W:CONV_W + FOX_W, :], preferred_element_type=F32)
    acc = acc + jnp.dot(ym_ref[...], w_ref[CONV_W + FOX_W:, :], preferred_element_type=F32)
    r = x_ref[...] + acc
    ms = jnp.mean(r * r, axis=-1, keepdims=True)
    o_ref[...] = r * lax.rsqrt(ms + EPS) * g_ref[...]


def _out_proj(x, y_conv, y_fox, y_mem, w_out, final_g, *, tm):
    s, d = x.shape
    row = lambda w: pl.BlockSpec((tm, w), lambda i: (i, 0))
    return pl.pallas_call(
        _out_kernel,
        grid=(s // tm,),
        in_specs=[
            row(d), row(CONV_W), row(FOX_W), row(MEM_W),
            pl.BlockSpec(w_out.shape, lambda i: (0, 0)),
            pl.BlockSpec((1, d), lambda i: (0, 0)),
        ],
        out_specs=row(d),
        out_shape=jax.ShapeDtypeStruct((s, d), F32),
        compiler_params=_params("arbitrary"),
        name="out_proj",
    )(x, y_conv, y_fox, y_mem, w_out, final_g)


def _layer(x, mem, norm_g, mem_norm_g, w_in, b_f, conv_w, conv_b, ln_g, ln_b, w_pw, w_kv, w_out):
    d = x.shape[-1]
    n_in_front = 3 * CONV_W + 3 * FOX_W
    w_main = jnp.concatenate(
        [w_in[:, :n_in_front], w_in[:, n_in_front + N_FOX_HEADS:]], axis=1).astype(BF16)
    w_f = jnp.pad(w_in[:, n_in_front:n_in_front + N_FOX_HEADS],
                  ((0, 0), (0, LANES - N_FOX_HEADS))).astype(BF16)
    col_scale = jnp.ones((1, N_MAIN), F32).at[:, COL_Q:COL_K].set(HEAD_DIM ** -0.5 * LOG2E)
    b_f_row = jnp.pad(b_f, (0, LANES - N_FOX_HEADS)).reshape(1, LANES)

    proj, flog = _in_proj(x, norm_g.reshape(1, d), w_main, w_f, col_scale, tm=1024, tn=1664)
    c_hs = _decay(flog, b_f_row)
    y_conv = _conv_branch(proj, conv_w, conv_b.reshape(1, -1), ln_g.reshape(1, -1), ln_b.reshape(1, -1),
                          w_pw.astype(BF16), tm=512)
    y_fox = _fox_attention(proj, c_hs.reshape(N_FOX_HEADS, 1, -1), tq=512, heads=4)
    y_mem = _mem_attention(proj, mem, mem_norm_g.reshape(1, d), w_kv.astype(BF16), tm=1024)
    return y_conv, y_fox, y_mem


def kernel(x, mem, norm_g, mem_norm_g, w_in, b_f, conv_w, conv_b, conv_ln_g, conv_ln_b, w_conv_pw,
           w_mem_kv, w_out, final_g):
    b, s, d = x.shape
    assert w_in.shape[0] == 1, "a single trunk layer is supported"
    outs = []
    for bi in range(b):
        y_conv, y_fox, y_mem = _layer(
            x[bi], mem[bi], norm_g[0], mem_norm_g[0], w_in[0], b_f[0], conv_w[0], conv_b[0],
            conv_ln_g[0], conv_ln_b[0], w_conv_pw[0], w_mem_kv[0], w_out[0])
        outs.append(_out_proj(x[bi], y_conv, y_fox, y_mem, w_out[0].astype(BF16),
                              final_g.reshape(1, d), tm=512))
    return jnp.stack(outs, axis=0)
```

```python
import functools

import jax
import jax.numpy as jnp
from jax import lax
from jax.experimental import pallas as pl
from jax.experimental.pallas import tpu as pltpu

F32 = jnp.float32
BF16 = jnp.bfloat16

LANES = 128
HEAD_DIM = 128
N_FOX_HEADS = 8
N_MEM_HEADS = 4
CONV_W = 512
FOX_W = N_FOX_HEADS * HEAD_DIM
MEM_W = N_MEM_HEADS * HEAD_DIM
CONV_K = 31
CONV_HALO = 32
EPS = 1e-6
MASK_VALUE = -1e30
LOG2E = 1.4426950408889634

COL_Q = 0
COL_K = COL_Q + FOX_W
COL_V = COL_K + FOX_W
COL_FOX_GATE = COL_V + FOX_W
COL_CONV_A = COL_FOX_GATE + FOX_W
COL_CONV_B = COL_CONV_A + CONV_W
COL_CONV_GATE = COL_CONV_B + CONV_W
COL_MQ = COL_CONV_GATE + CONV_W
COL_MEM_GATE = COL_MQ + MEM_W
N_MAIN = COL_MEM_GATE + MEM_W

FOX_TILE = 512
SKIP_MARGIN = 160.0
BOUND_SLACK = 2.0 ** -7

VMEM_LIMIT_BYTES = 56 * 1024 * 1024


def _params(*sem):
    return pltpu.CompilerParams(dimension_semantics=sem, vmem_limit_bytes=VMEM_LIMIT_BYTES)


def _silu(x):
    return x * jax.nn.sigmoid(x)


def _in_proj_kernel(x_ref, g_ref, w_ref, wf_ref, cs_ref, o_ref, flog_ref, h_ref, *, row_chunk):
    n = pl.program_id(1)
    tm = x_ref.shape[0]

    @pl.when(n == 0)
    def _():
        def chunk(ci, carry):
            r = pl.multiple_of(ci * row_chunk, row_chunk)
            x = x_ref[pl.ds(r, row_chunk), :]
            ms = jnp.mean(x * x, axis=-1, keepdims=True)
            h = x * lax.rsqrt(ms + EPS) * g_ref[...]
            h_ref[pl.ds(r, row_chunk), :] = h.astype(BF16)
            return carry

        lax.fori_loop(0, tm // row_chunk, chunk, 0)
        flog_ref[...] = jnp.dot(h_ref[...], wf_ref[...], preferred_element_type=F32)

    acc = jnp.dot(h_ref[...], w_ref[...], preferred_element_type=F32)
    o_ref[...] = (acc * cs_ref[...]).astype(BF16)


def _in_proj(x, norm_g, w_main, w_f, col_scale, *, tm, tn):
    s, d = x.shape
    n_main = w_main.shape[1]
    return pl.pallas_call(
        functools.partial(_in_proj_kernel, row_chunk=128),
        grid=(s // tm, n_main // tn),
        in_specs=[
            pl.BlockSpec((tm, d), lambda i, n: (i, 0)),
            pl.BlockSpec((1, d), lambda i, n: (0, 0)),
            pl.BlockSpec((d, tn), lambda i, n: (0, n)),
            pl.BlockSpec((d, LANES), lambda i, n: (0, 0)),
            pl.BlockSpec((1, tn), lambda i, n: (0, n)),
        ],
        out_specs=[
            pl.BlockSpec((tm, tn), lambda i, n: (i, n)),
            pl.BlockSpec((tm, LANES), lambda i, n: (i, 0)),
        ],
        out_shape=[
            jax.ShapeDtypeStruct((s, n_main), BF16),
            jax.ShapeDtypeStruct((s, LANES), F32),
        ],
        scratch_shapes=[pltpu.VMEM((tm, d), BF16)],
        compiler_params=_params("arbitrary", "arbitrary"),
        name="in_proj",
    )(x, norm_g, w_main, w_f, col_scale)


def _decay_kernel(flog_ref, bf_ref, c_hs_ref, c_sh_ref):
    s = flog_ref.shape[0]
    logf = jax.nn.log_sigmoid(flog_ref[...] + bf_ref[...])
    c = logf.T[:N_FOX_HEADS, :]
    pos = lax.broadcasted_iota(jnp.int32, c.shape, 1)
    shift = 1
    while shift < s:
        c = c + jnp.where(pos >= shift, pltpu.roll(c, shift, 1), 0.0)
        shift *= 2
    c = c * LOG2E
    c_hs_ref[...] = c
    c_sh_ref[...] = jnp.concatenate([c, jnp.zeros((LANES - N_FOX_HEADS, s), F32)], axis=0).T


def _decay(flog, b_f_row):
    s = flog.shape[0]
    return pl.pallas_call(
        _decay_kernel,
        out_shape=[
            jax.ShapeDtypeStruct((N_FOX_HEADS, s), F32),
            jax.ShapeDtypeStruct((s, LANES), F32),
        ],
        compiler_params=pltpu.CompilerParams(vmem_limit_bytes=VMEM_LIMIT_BYTES),
        name="decay_cumsum",
    )(flog, b_f_row)


def _conv_kernel(a_ref, b_ref, ap_ref, bp_ref, gate_ref, cw_ref, cb_ref, lg_ref, lb_ref, pw_ref,
                 o_ref, u_ref, *, row_chunk):
    i = pl.program_id(0)
    tm = a_ref.shape[0]

    prev = ap_ref[...].astype(F32) * jax.nn.sigmoid(bp_ref[...].astype(F32))
    u_ref[0:CONV_HALO, :] = jnp.where(i > 0, prev, 0.0)
    u_ref[CONV_HALO:CONV_HALO + tm, :] = a_ref[...].astype(F32) * jax.nn.sigmoid(b_ref[...].astype(F32))

    first_tap = CONV_HALO - (CONV_K - 1)

    def chunk(ci, carry):
        r = pl.multiple_of(ci * row_chunk, row_chunk)
        win = u_ref[pl.ds(r, row_chunk + CONV_HALO), :]
        acc = jnp.zeros((row_chunk, CONV_W), F32) + cb_ref[...]
        for k in range(CONV_K):
            acc = acc + cw_ref[k:k + 1, :] * win[first_tap + k:first_tap + k + row_chunk, :]
        mu = jnp.mean(acc, axis=-1, keepdims=True)
        cen = acc - mu
        var = jnp.mean(cen * cen, axis=-1, keepdims=True)
        y = cen * lax.rsqrt(var + EPS) * lg_ref[...] + lb_ref[...]
        y = _silu(y).astype(BF16)
        pw = jnp.dot(y, pw_ref[...], preferred_element_type=F32)
        gate = gate_ref[pl.ds(r, row_chunk), :].astype(F32)
        o_ref[pl.ds(r, row_chunk), :] = (pw * _silu(gate)).astype(BF16)
        return carry

    lax.fori_loop(0, tm // row_chunk, chunk, 0)


def _conv_branch(proj, conv_w, conv_b, ln_g, ln_b, w_pw, *, tm):
    s = proj.shape[0]
    halo_blocks = tm // CONV_HALO
    col = lambda c: c // CONV_W

    def prev_map(c):
        return lambda i: (jnp.maximum(i * halo_blocks - 1, 0), col(c))

    full = lambda shape: pl.BlockSpec(shape, lambda i: (0, 0))
    return pl.pallas_call(
        functools.partial(_conv_kernel, row_chunk=64),
        grid=(s // tm,),
        in_specs=[
            pl.BlockSpec((tm, CONV_W), lambda i: (i, col(COL_CONV_A))),
            pl.BlockSpec((tm, CONV_W), lambda i: (i, col(COL_CONV_B))),
            pl.BlockSpec((CONV_HALO, CONV_W), prev_map(COL_CONV_A)),
            pl.BlockSpec((CONV_HALO, CONV_W), prev_map(COL_CONV_B)),
            pl.BlockSpec((tm, CONV_W), lambda i: (i, col(COL_CONV_GATE))),
            full((CONV_K, CONV_W)),
            full((1, CONV_W)),
            full((1, CONV_W)),
            full((1, CONV_W)),
            full((CONV_W, CONV_W)),
        ],
        out_specs=pl.BlockSpec((tm, CONV_W), lambda i: (i, 0)),
        out_shape=jax.ShapeDtypeStruct((s, CONV_W), BF16),
        scratch_shapes=[pltpu.VMEM((CONV_HALO + tm, CONV_W), F32)],
        compiler_params=_params("arbitrary"),
        name="conv_branch",
    )(proj, proj, proj, proj, proj, conv_w, conv_b, ln_g, ln_b, w_pw)


def _fox_kernel(noff_ref, q_ref, k_ref, v_ref, crow_ref, gate_ref, o_ref, m_ref, l_ref, acc_ref, *, heads):
    hp = pl.program_id(0)
    i = pl.program_id(1)
    tq = q_ref.shape[0]
    tk = tq
    lane_reps = tk // LANES

    q_start = pl.multiple_of(i * tq, tq)
    cbase = [crow_ref[hp * heads + g, :, pl.ds(q_start, LANES)][:, 0:1] for g in range(heads)]

    m_ref[...] = jnp.full(m_ref.shape, MASK_VALUE, F32)
    l_ref[...] = jnp.zeros(l_ref.shape, F32)
    acc_ref[...] = jnp.zeros(acc_ref.shape, F32)

    def step(j, masked):
        start = pl.multiple_of(j * tk, tk)
        for g in range(heads):
            lo = g * HEAD_DIM
            q = q_ref[:, lo:lo + HEAD_DIM]
            k = k_ref[pl.ds(start, tk), lo:lo + HEAD_DIM]
            v = v_ref[pl.ds(start, tk), lo:lo + HEAD_DIM]
            ck = crow_ref[hp * heads + g, :, pl.ds(start, tk)] - cbase[g]
            s = lax.dot_general(q, k, (((1,), (1,)), ((), ())), preferred_element_type=F32)
            s = s - ck
            if masked:
                row = lax.broadcasted_iota(jnp.int32, s.shape, 0)
                colp = lax.broadcasted_iota(jnp.int32, s.shape, 1)
                s = jnp.where(colp <= row, s, MASK_VALUE)
            m_old = m_ref[g]
            m_new = jnp.maximum(m_old, jnp.max(s, axis=1, keepdims=True))
            alpha = jnp.exp2(m_old - m_new)
            p = jnp.exp2(s - jnp.concatenate([m_new] * lane_reps, axis=1))
            l_ref[g] = alpha * l_ref[g] + jnp.sum(p, axis=1, keepdims=True)
            acc_ref[g] = alpha * acc_ref[g] + jnp.dot(p.astype(BF16), v, preferred_element_type=F32)
            m_ref[g] = m_new

    step(i, True)

    def body(jj, carry):
        step(i - 1 - jj, False)
        return carry

    n_off = noff_ref[hp * heads, i]
    for g in range(1, heads):
        n_off = jnp.maximum(n_off, noff_ref[hp * heads + g, i])
    lax.fori_loop(0, n_off, body, 0)

    for g in range(heads):
        lo = g * HEAD_DIM
        gate = gate_ref[:, lo:lo + HEAD_DIM].astype(F32)
        o_ref[:, lo:lo + HEAD_DIM] = (acc_ref[g] / l_ref[g] * _silu(gate)).astype(BF16)


def _fox_attention(n_off, proj, c_hs, *, tq, heads):
    s = proj.shape[0]
    w = heads * HEAD_DIM
    gcol = lambda c: c // w
    grid_spec = pltpu.PrefetchScalarGridSpec(
        num_scalar_prefetch=1,
        grid=(N_FOX_HEADS // heads, s // tq),
        in_specs=[
            pl.BlockSpec((tq, w), lambda h, i, n: (i, gcol(COL_Q) + h)),
            pl.BlockSpec((s, w), lambda h, i, n: (0, gcol(COL_K) + h)),
            pl.BlockSpec((s, w), lambda h, i, n: (0, gcol(COL_V) + h)),
            pl.BlockSpec((N_FOX_HEADS, 1, s), lambda h, i, n: (0, 0, 0)),
            pl.BlockSpec((tq, w), lambda h, i, n: (i, gcol(COL_FOX_GATE) + h)),
        ],
        out_specs=pl.BlockSpec((tq, w), lambda h, i, n: (i, h)),
        scratch_shapes=[
            pltpu.VMEM((heads, tq, LANES), F32),
            pltpu.VMEM((heads, tq, LANES), F32),
            pltpu.VMEM((heads, tq, HEAD_DIM), F32),
        ],
    )
    return pl.pallas_call(
        functools.partial(_fox_kernel, heads=heads),
        grid_spec=grid_spec,
        out_shape=jax.ShapeDtypeStruct((s, FOX_W), BF16),
        compiler_params=_params("arbitrary", "arbitrary"),
        name="fox_attention",
    )(n_off, proj, proj, proj, c_hs, proj)


def _bounds_kernel(q_ref, k_ref, c_ref, o_ref, e_ref, qmax_ref, kmax_ref, dmin_ref, cfirst_ref, clast_ref):
    i = pl.program_id(0)
    nq = pl.num_programs(0)
    tq = q_ref.shape[0]

    @pl.when(i == 0)
    def _():
        col_head = lax.broadcasted_iota(jnp.int32, e_ref.shape, 0) // HEAD_DIM
        head = lax.broadcasted_iota(jnp.int32, e_ref.shape, 1)
        e_ref[...] = jnp.where(col_head == head, 1.0, 0.0).astype(BF16)

    q = q_ref[...].astype(F32)
    k = k_ref[...].astype(F32)
    e = e_ref[...]
    head_sum = lambda x: jnp.dot(x.astype(BF16), e, preferred_element_type=F32)
    qn = jnp.sqrt(head_sum(q * q))
    kn = jnp.sqrt(head_sum(k * k))
    d = head_sum(q * k) - BOUND_SLACK * qn * kn
    rep = lambda x: jnp.broadcast_to(x, (8, LANES))
    qmax_ref[i] = rep(jnp.max(qn, axis=0, keepdims=True))
    kmax_ref[i] = rep(jnp.max(kn, axis=0, keepdims=True))
    dmin_ref[i] = rep(jnp.min(d, axis=0, keepdims=True))
    cfirst_ref[i] = rep(c_ref[0:1, :])
    clast_ref[i] = rep(c_ref[tq - 1:tq, :])

    @pl.when(i == nq - 1)
    def _():
        kmax = jnp.max(kmax_ref[...], axis=0, keepdims=True)
        reach = qmax_ref[...] * kmax * (1.0 + BOUND_SLACK) - dmin_ref[...] + SKIP_MARGIN
        cfirst = cfirst_ref[...]
        tile = lax.broadcasted_iota(jnp.int32, cfirst.shape, 0)
        count = jnp.zeros(cfirst.shape, jnp.int32)
        for j in range(o_ref.shape[0]):
            needed = (clast_ref[j:j + 1] - cfirst <= reach) & (tile > j)
            count = count + needed.astype(jnp.int32)
        o_ref[...] = count


def _fox_bounds(proj, c_sh, *, tq):
    s = proj.shape[0]
    nq = s // tq
    stat = pltpu.VMEM((nq, 8, LANES), F32)
    return pl.pallas_call(
        _bounds_kernel,
        grid=(nq,),
        in_specs=[
            pl.BlockSpec((tq, FOX_W), lambda i: (i, COL_Q // FOX_W)),
            pl.BlockSpec((tq, FOX_W), lambda i: (i, COL_K // FOX_W)),
            pl.BlockSpec((tq, LANES), lambda i: (i, 0)),
        ],
        out_specs=pl.BlockSpec((nq, 8, LANES), lambda i: (0, 0, 0)),
        out_shape=jax.ShapeDtypeStruct((nq, 8, LANES), jnp.int32),
        scratch_shapes=[pltpu.VMEM((FOX_W, LANES), BF16), stat, stat, stat, stat, stat],
        compiler_params=_params("arbitrary"),
        name="fox_bounds",
    )(proj, proj, c_sh)


def _mem_kernel(mem_ref, mg_ref, wkv_ref, q_ref, gate_ref, o_ref, kv_ref):
    i = pl.program_id(0)

    @pl.when(i == 0)
    def _():
        mem = mem_ref[...]
        ms = jnp.mean(mem * mem, axis=-1, keepdims=True)
        hm = (mem * lax.rsqrt(ms + EPS) * mg_ref[...]).astype(BF16)
        kv_ref[...] = jnp.dot(hm, wkv_ref[...], preferred_element_type=F32).astype(BF16)

    scale = HEAD_DIM ** -0.5
    for hh in range(N_MEM_HEADS):
        lo = hh * HEAD_DIM
        q = q_ref[:, lo:lo + HEAD_DIM]
        k = kv_ref[:, lo:lo + HEAD_DIM]
        v = kv_ref[:, MEM_W + lo:MEM_W + lo + HEAD_DIM]
        s = lax.dot_general(q, k, (((1,), (1,)), ((), ())), preferred_element_type=F32) * scale
        m = jnp.max(s, axis=1, keepdims=True)
        p = jnp.exp(s - m)
        l = jnp.sum(p, axis=1, keepdims=True)
        pv = jnp.dot(p.astype(BF16), v, preferred_element_type=F32)
        gate = gate_ref[:, lo:lo + HEAD_DIM].astype(F32)
        o_ref[:, lo:lo + HEAD_DIM] = (pv / l * _silu(gate)).astype(BF16)


def _mem_attention(proj, mem, mem_norm_g, w_kv, *, tm):
    s = proj.shape[0]
    m_len, d = mem.shape
    col = lambda c: c // MEM_W
    full = lambda shape: pl.BlockSpec(shape, lambda i: (0, 0))
    return pl.pallas_call(
        _mem_kernel,
        grid=(s // tm,),
        in_specs=[
            full((m_len, d)),
            full((1, d)),
            full((d, 2 * MEM_W)),
            pl.BlockSpec((tm, MEM_W), lambda i: (i, col(COL_MQ))),
            pl.BlockSpec((tm, MEM_W), lambda i: (i, col(COL_MEM_GATE))),
        ],
        out_specs=pl.BlockSpec((tm, MEM_W), lambda i: (i, 0)),
        out_shape=jax.ShapeDtypeStruct((s, MEM_W), BF16),
        scratch_shapes=[pltpu.VMEM((m_len, 2 * MEM_W), BF16)],
        compiler_params=_params("arbitrary"),
        name="mem_attention",
    )(mem, mem_norm_g, w_kv, proj, proj)


def _out_kernel(x_ref, yc_ref, yf_ref, ym_ref, w_ref, g_ref, o_ref):
    acc = jnp.dot(yc_ref[...], w_ref[0:CONV_W, :], preferred_element_type=F32)
    acc = acc + jnp.dot(yf_ref[...], w_ref[CONV_W:CONV_W + FOX_W, :], preferred_element_type=F32)
    acc = acc + jnp.dot(ym_ref[...], w_ref[CONV_W + FOX_W:, :], preferred_element_type=F32)
    r = x_ref[...] + acc
    ms = jnp.mean(r * r, axis=-1, keepdims=True)
    o_ref[...] = r * lax.rsqrt(ms + EPS) * g_ref[...]


def _out_proj(x, y_conv, y_fox, y_mem, w_out, final_g, *, tm):
    s, d = x.shape
    row = lambda w: pl.BlockSpec((tm, w), lambda i: (i, 0))
    return pl.pallas_call(
        _out_kernel,
        grid=(s // tm,),
        in_specs=[
            row(d), row(CONV_W), row(FOX_W), row(MEM_W),
            pl.BlockSpec(w_out.shape, lambda i: (0, 0)),
            pl.BlockSpec((1, d), lambda i: (0, 0)),
        ],
        out_specs=row(d),
        out_shape=jax.ShapeDtypeStruct((s, d), F32),
        compiler_params=_params("arbitrary"),
        name="out_proj",
    )(x, y_conv, y_fox, y_mem, w_out, final_g)


def _layer(x, mem, norm_g, mem_norm_g, w_in, b_f, conv_w, conv_b, ln_g, ln_b, w_pw, w_kv, w_out):
    d = x.shape[-1]
    conv_end = 3 * CONV_W
    qkv_end = conv_end + 3 * FOX_W
    gate_start = qkv_end + N_FOX_HEADS
    gate_end = gate_start + FOX_W
    w_main = jnp.concatenate(
        [w_in[:, conv_end:qkv_end], w_in[:, gate_start:gate_end], w_in[:, :conv_end], w_in[:, gate_end:]],
        axis=1).astype(BF16)
    w_f = jnp.pad(w_in[:, qkv_end:gate_start], ((0, 0), (0, LANES - N_FOX_HEADS))).astype(BF16)
    col_scale = jnp.ones((1, N_MAIN), F32).at[:, COL_Q:COL_K].set(HEAD_DIM ** -0.5 * LOG2E)
    b_f_row = jnp.pad(b_f, (0, LANES - N_FOX_HEADS)).reshape(1, LANES)

    proj, flog = _in_proj(x, norm_g.reshape(1, d), w_main, w_f, col_scale, tm=1024, tn=1664)
    c_hs, c_sh = _decay(flog, b_f_row)
    y_conv = _conv_branch(proj, conv_w, conv_b.reshape(1, -1), ln_g.reshape(1, -1), ln_b.reshape(1, -1),
                          w_pw.astype(BF16), tm=512)
    n_off = _fox_bounds(proj, c_sh, tq=FOX_TILE)[:, 0, :N_FOX_HEADS].T
    y_fox = _fox_attention(n_off, proj, c_hs.reshape(N_FOX_HEADS, 1, -1), tq=FOX_TILE, heads=4)
    y_mem = _mem_attention(proj, mem, mem_norm_g.reshape(1, d), w_kv.astype(BF16), tm=1024)
    return y_conv, y_fox, y_mem


def kernel(x, mem, norm_g, mem_norm_g, w_in, b_f, conv_w, conv_b, conv_ln_g, conv_ln_b, w_conv_pw,
           w_mem_kv, w_out, final_g):
    b, s, d = x.shape
    assert w_in.shape[0] == 1, "a single trunk layer is supported"
    outs = []
    for bi in range(b):
        y_conv, y_fox, y_mem = _layer(
            x[bi], mem[bi], norm_g[0], mem_norm_g[0], w_in[0], b_f[0], conv_w[0], conv_b[0],
            conv_ln_g[0], conv_ln_b[0], w_conv_pw[0], w_mem_kv[0], w_out[0])
        outs.append(_out_proj(x[bi], y_conv, y_fox, y_mem, w_out[0].astype(BF16),
                              final_g.reshape(1, d), tm=512))
    return jnp.stack(outs, axis=0)
```

```python
import functools

import jax
import jax.numpy as jnp
from jax import lax
from jax.experimental import pallas as pl
from jax.experimental.pallas import tpu as pltpu

F32 = jnp.float32
BF16 = jnp.bfloat16

LANES = 128
SUBLANES = 8
HEAD_DIM = 128
N_FOX_HEADS = 8
N_MEM_HEADS = 4
CONV_W = 512
FOX_W = N_FOX_HEADS * HEAD_DIM
MEM_W = N_MEM_HEADS * HEAD_DIM
CONV_K = 31
CONV_HALO = 32
EPS = 1e-6
MASK_VALUE = -1e30
LOG2E = 1.4426950408889634

COL_Q = 0
COL_K = COL_Q + FOX_W
COL_V = COL_K + FOX_W
COL_FOX_GATE = COL_V + FOX_W
COL_CONV_A = COL_FOX_GATE + FOX_W
COL_CONV_B = COL_CONV_A + CONV_W
COL_CONV_GATE = COL_CONV_B + CONV_W
COL_MQ = COL_CONV_GATE + CONV_W
COL_MEM_GATE = COL_MQ + MEM_W
N_MAIN = COL_MEM_GATE + MEM_W

FOX_TILE = 512
SKIP_MARGIN = 160.0
BOUND_SLACK = 2.0 ** -7

VMEM_LIMIT_BYTES = 56 * 1024 * 1024


def _params(*sem):
    return pltpu.CompilerParams(dimension_semantics=sem, vmem_limit_bytes=VMEM_LIMIT_BYTES)


def _silu(x):
    return x * jax.nn.sigmoid(x)


REF_CONV_A = 0
REF_Q = 3 * CONV_W
REF_F_LOGIT = REF_Q + 3 * FOX_W
REF_FOX_GATE = REF_F_LOGIT + N_FOX_HEADS
REF_MQ = REF_FOX_GATE + FOX_W
WPREP_ROWS = 512


def _wprep_source_rows():
    segments = [(REF_Q, 3 * FOX_W), (REF_FOX_GATE, FOX_W), (REF_CONV_A, 3 * CONV_W), (REF_MQ, 2 * MEM_W)]
    rows = []
    for start, width in segments:
        rows += [start + r for r in range(0, width, WPREP_ROWS)]
    assert len(rows) * WPREP_ROWS == N_MAIN
    return rows


def _wprep_kernel(src_ref, w_ref, f_ref, wm_ref, wf_ref):
    del src_ref
    wm_ref[...] = w_ref[...].astype(BF16)

    @pl.when(pl.program_id(0) == 0)
    def _():
        row = lax.broadcasted_iota(jnp.int32, wf_ref.shape, 0)
        wf_ref[...] = jnp.where(row < N_FOX_HEADS, f_ref[...], 0.0).astype(BF16)


def _wprep(w_t):
    d = w_t.shape[1]
    src = jnp.asarray([r // SUBLANES for r in _wprep_source_rows()], jnp.int32)
    grid_spec = pltpu.PrefetchScalarGridSpec(
        num_scalar_prefetch=1,
        grid=(N_MAIN // WPREP_ROWS,),
        in_specs=[
            pl.BlockSpec((pl.Element(WPREP_ROWS), pl.Element(d)), lambda i, src: (src[i] * SUBLANES, 0)),
            pl.BlockSpec((pl.Element(LANES), pl.Element(d)), lambda i, src: (REF_F_LOGIT, 0)),
        ],
        out_specs=[
            pl.BlockSpec((WPREP_ROWS, d), lambda i, src: (i, 0)),
            pl.BlockSpec((LANES, d), lambda i, src: (0, 0)),
        ],
    )
    return pl.pallas_call(
        _wprep_kernel,
        grid_spec=grid_spec,
        out_shape=[
            jax.ShapeDtypeStruct((N_MAIN, d), BF16),
            jax.ShapeDtypeStruct((LANES, d), BF16),
        ],
        compiler_params=_params("arbitrary"),
        name="w_in_layout",
    )(src, w_t, w_t)


_NT = (((1,), (1,)), ((), ()))


def _in_proj_kernel(x_ref, g_ref, w_ref, wf_ref, cs_ref, o_ref, flog_ref, h_ref, *, row_chunk):
    n = pl.program_id(1)
    tm = x_ref.shape[0]

    @pl.when(n == 0)
    def _():
        def chunk(ci, carry):
            r = pl.multiple_of(ci * row_chunk, row_chunk)
            x = x_ref[pl.ds(r, row_chunk), :]
            ms = jnp.mean(x * x, axis=-1, keepdims=True)
            h = x * lax.rsqrt(ms + EPS) * g_ref[...]
            h_ref[pl.ds(r, row_chunk), :] = h.astype(BF16)
            return carry

        lax.fori_loop(0, tm // row_chunk, chunk, 0)
        flog_ref[...] = lax.dot_general(h_ref[...], wf_ref[...], _NT, preferred_element_type=F32)

    acc = lax.dot_general(h_ref[...], w_ref[...], _NT, preferred_element_type=F32)
    o_ref[...] = (acc * cs_ref[...]).astype(BF16)


def _in_proj(x, norm_g, w_main, w_f, col_scale, *, tm, tn):
    s, d = x.shape
    n_main = w_main.shape[0]
    return pl.pallas_call(
        functools.partial(_in_proj_kernel, row_chunk=128),
        grid=(s // tm, n_main // tn),
        in_specs=[
            pl.BlockSpec((tm, d), lambda i, n: (i, 0)),
            pl.BlockSpec((1, d), lambda i, n: (0, 0)),
            pl.BlockSpec((tn, d), lambda i, n: (n, 0)),
            pl.BlockSpec((LANES, d), lambda i, n: (0, 0)),
            pl.BlockSpec((1, tn), lambda i, n: (0, n)),
        ],
        out_specs=[
            pl.BlockSpec((tm, tn), lambda i, n: (i, n)),
            pl.BlockSpec((tm, LANES), lambda i, n: (i, 0)),
        ],
        out_shape=[
            jax.ShapeDtypeStruct((s, n_main), BF16),
            jax.ShapeDtypeStruct((s, LANES), F32),
        ],
        scratch_shapes=[pltpu.VMEM((tm, d), BF16)],
        compiler_params=_params("arbitrary", "arbitrary"),
        name="in_proj",
    )(x, norm_g, w_main, w_f, col_scale)


def _decay_kernel(flog_ref, bf_ref, c_hs_ref, c_sh_ref):
    s = flog_ref.shape[0]
    logf = jax.nn.log_sigmoid(flog_ref[...] + bf_ref[...])
    c = logf.T[:N_FOX_HEADS, :]
    pos = lax.broadcasted_iota(jnp.int32, c.shape, 1)
    shift = 1
    while shift < s:
        c = c + jnp.where(pos >= shift, pltpu.roll(c, shift, 1), 0.0)
        shift *= 2
    c = c * LOG2E
    c_hs_ref[...] = c
    c_sh_ref[...] = jnp.concatenate([c, jnp.zeros((LANES - N_FOX_HEADS, s), F32)], axis=0).T


def _decay(flog, b_f_row):
    s = flog.shape[0]
    return pl.pallas_call(
        _decay_kernel,
        out_shape=[
            jax.ShapeDtypeStruct((N_FOX_HEADS, s), F32),
            jax.ShapeDtypeStruct((s, LANES), F32),
        ],
        compiler_params=pltpu.CompilerParams(vmem_limit_bytes=VMEM_LIMIT_BYTES),
        name="decay_cumsum",
    )(flog, b_f_row)


def _conv_kernel(a_ref, b_ref, ap_ref, bp_ref, gate_ref, cw_ref, cb_ref, lg_ref, lb_ref, pw_ref,
                 o_ref, u_ref, c_ref, *, row_chunk):
    i = pl.program_id(0)
    tm = a_ref.shape[0]

    shifted_rows = tm + CONV_HALO - SUBLANES
    prev = ap_ref[...].astype(F32) * jax.nn.sigmoid(bp_ref[...].astype(F32))
    u_ref[0, 0:CONV_HALO, :] = jnp.where(i > 0, prev, 0.0)
    u_ref[0, CONV_HALO:CONV_HALO + tm, :] = a_ref[...].astype(F32) * jax.nn.sigmoid(b_ref[...].astype(F32))
    for p in range(1, SUBLANES):
        u_ref[p, 0:shifted_rows, :] = u_ref[0, p:p + shifted_rows, :]

    first_tap = CONV_HALO - (CONV_K - 1)

    def chunk(ci, carry):
        r = pl.multiple_of(ci * row_chunk, row_chunk)
        acc = jnp.zeros((row_chunk, CONV_W), F32) + cb_ref[...]
        for k in range(CONV_K):
            phase, base = (first_tap + k) % SUBLANES, (first_tap + k) // SUBLANES * SUBLANES
            tap = u_ref[phase, pl.ds(pl.multiple_of(r + base, SUBLANES), row_chunk), :]
            acc = acc + cw_ref[k:k + 1, :] * tap
        c_ref[pl.ds(r, row_chunk), :] = acc
        return carry

    lax.fori_loop(0, tm // row_chunk, chunk, 0)

    conv = c_ref[...]
    mu = jnp.mean(conv, axis=-1, keepdims=True)
    cen = conv - mu
    var = jnp.mean(cen * cen, axis=-1, keepdims=True)
    y = cen * lax.rsqrt(var + EPS) * lg_ref[...] + lb_ref[...]
    y = _silu(y).astype(BF16)
    pw = jnp.dot(y, pw_ref[...], preferred_element_type=F32)
    o_ref[...] = (pw * _silu(gate_ref[...].astype(F32))).astype(BF16)


def _conv_branch(proj, conv_w, conv_b, ln_g, ln_b, w_pw, *, tm):
    s = proj.shape[0]
    halo_blocks = tm // CONV_HALO
    col = lambda c: c // CONV_W

    def prev_map(c):
        return lambda i: (jnp.maximum(i * halo_blocks - 1, 0), col(c))

    full = lambda shape: pl.BlockSpec(shape, lambda i: (0, 0))
    return pl.pallas_call(
        functools.partial(_conv_kernel, row_chunk=64),
        grid=(s // tm,),
        in_specs=[
            pl.BlockSpec((tm, CONV_W), lambda i: (i, col(COL_CONV_A))),
            pl.BlockSpec((tm, CONV_W), lambda i: (i, col(COL_CONV_B))),
            pl.BlockSpec((CONV_HALO, CONV_W), prev_map(COL_CONV_A)),
            pl.BlockSpec((CONV_HALO, CONV_W), prev_map(COL_CONV_B)),
            pl.BlockSpec((tm, CONV_W), lambda i: (i, col(COL_CONV_GATE))),
            full((CONV_K, CONV_W)),
            full((1, CONV_W)),
            full((1, CONV_W)),
            full((1, CONV_W)),
            full((CONV_W, CONV_W)),
        ],
        out_specs=pl.BlockSpec((tm, CONV_W), lambda i: (i, 0)),
        out_shape=jax.ShapeDtypeStruct((s, CONV_W), BF16),
        scratch_shapes=[pltpu.VMEM((SUBLANES, CONV_HALO + tm, CONV_W), F32), pltpu.VMEM((tm, CONV_W), F32)],
        compiler_params=_params("arbitrary"),
        name="conv_branch",
    )(proj, proj, proj, proj, proj, conv_w, conv_b, ln_g, ln_b, w_pw)


def _fox_kernel(noff_ref, q_ref, k_ref, v_ref, crow_ref, gate_ref, o_ref, m_ref, l_ref, acc_ref, *, heads):
    hp = pl.program_id(0)
    i = pl.program_id(1)
    tq = q_ref.shape[0]
    tk = tq
    lane_reps = tk // LANES

    q_start = pl.multiple_of(i * tq, tq)
    cbase = [crow_ref[hp * heads + g, :, pl.ds(q_start, LANES)][:, 0:1] for g in range(heads)]

    m_ref[...] = jnp.full(m_ref.shape, MASK_VALUE, F32)
    l_ref[...] = jnp.zeros(l_ref.shape, F32)
    acc_ref[...] = jnp.zeros(acc_ref.shape, F32)

    def step(j, masked):
        start = pl.multiple_of(j * tk, tk)
        for g in range(heads):
            lo = g * HEAD_DIM
            q = q_ref[:, lo:lo + HEAD_DIM]
            k = k_ref[pl.ds(start, tk), lo:lo + HEAD_DIM]
            v = v_ref[pl.ds(start, tk), lo:lo + HEAD_DIM]
            ck = crow_ref[hp * heads + g, :, pl.ds(start, tk)] - cbase[g]
            s = lax.dot_general(q, k, (((1,), (1,)), ((), ())), preferred_element_type=F32)
            s = s - ck
            if masked:
                row = lax.broadcasted_iota(jnp.int32, s.shape, 0)
                colp = lax.broadcasted_iota(jnp.int32, s.shape, 1)
                s = jnp.where(colp <= row, s, MASK_VALUE)
            m_old = m_ref[g]
            m_new = jnp.maximum(m_old, jnp.max(s, axis=1, keepdims=True))
            alpha = jnp.exp2(m_old - m_new)
            p = jnp.exp2(s - jnp.concatenate([m_new] * lane_reps, axis=1))
            l_ref[g] = alpha * l_ref[g] + jnp.sum(p, axis=1, keepdims=True)
            acc_ref[g] = alpha * acc_ref[g] + jnp.dot(p.astype(BF16), v, preferred_element_type=F32)
            m_ref[g] = m_new

    step(i, True)

    def body(jj, carry):
        step(i - 1 - jj, False)
        return carry

    n_off = noff_ref[hp * heads, i]
    for g in range(1, heads):
        n_off = jnp.maximum(n_off, noff_ref[hp * heads + g, i])
    lax.fori_loop(0, n_off, body, 0)

    for g in range(heads):
        lo = g * HEAD_DIM
        gate = gate_ref[:, lo:lo + HEAD_DIM].astype(F32)
        o_ref[:, lo:lo + HEAD_DIM] = (acc_ref[g] / l_ref[g] * _silu(gate)).astype(BF16)


def _fox_attention(n_off, proj, c_hs, *, tq, heads):
    s = proj.shape[0]
    w = heads * HEAD_DIM
    gcol = lambda c: c // w
    grid_spec = pltpu.PrefetchScalarGridSpec(
        num_scalar_prefetch=1,
        grid=(N_FOX_HEADS // heads, s // tq),
        in_specs=[
            pl.BlockSpec((tq, w), lambda h, i, n: (i, gcol(COL_Q) + h)),
            pl.BlockSpec((s, w), lambda h, i, n: (0, gcol(COL_K) + h)),
            pl.BlockSpec((s, w), lambda h, i, n: (0, gcol(COL_V) + h)),
            pl.BlockSpec((N_FOX_HEADS, 1, s), lambda h, i, n: (0, 0, 0)),
            pl.BlockSpec((tq, w), lambda h, i, n: (i, gcol(COL_FOX_GATE) + h)),
        ],
        out_specs=pl.BlockSpec((tq, w), lambda h, i, n: (i, h)),
        scratch_shapes=[
            pltpu.VMEM((heads, tq, LANES), F32),
            pltpu.VMEM((heads, tq, LANES), F32),
            pltpu.VMEM((heads, tq, HEAD_DIM), F32),
        ],
    )
    return pl.pallas_call(
        functools.partial(_fox_kernel, heads=heads),
        grid_spec=grid_spec,
        out_shape=jax.ShapeDtypeStruct((s, FOX_W), BF16),
        compiler_params=_params("arbitrary", "arbitrary"),
        name="fox_attention",
    )(n_off, proj, proj, proj, c_hs, proj)


def _bounds_kernel(q_ref, k_ref, c_ref, o_ref, e_ref, qmax_ref, kmax_ref, dmin_ref, cfirst_ref, clast_ref):
    i = pl.program_id(0)
    nq = pl.num_programs(0)
    tq = q_ref.shape[0]

    @pl.when(i == 0)
    def _():
        col_head = lax.broadcasted_iota(jnp.int32, e_ref.shape, 0) // HEAD_DIM
        head = lax.broadcasted_iota(jnp.int32, e_ref.shape, 1)
        e_ref[...] = jnp.where(col_head == head, 1.0, 0.0).astype(BF16)

    q = q_ref[...].astype(F32)
    k = k_ref[...].astype(F32)
    e = e_ref[...]
    head_sum = lambda x: jnp.dot(x.astype(BF16), e, preferred_element_type=F32)
    qn = jnp.sqrt(head_sum(q * q))
    kn = jnp.sqrt(head_sum(k * k))
    d = head_sum(q * k) - BOUND_SLACK * qn * kn
    rep = lambda x: jnp.broadcast_to(x, (8, LANES))
    qmax_ref[i] = rep(jnp.max(qn, axis=0, keepdims=True))
    kmax_ref[i] = rep(jnp.max(kn, axis=0, keepdims=True))
    dmin_ref[i] = rep(jnp.min(d, axis=0, keepdims=True))
    cfirst_ref[i] = rep(c_ref[0:1, :])
    clast_ref[i] = rep(c_ref[tq - 1:tq, :])

    @pl.when(i == nq - 1)
    def _():
        kmax = jnp.max(kmax_ref[...], axis=0, keepdims=True)
        reach = qmax_ref[...] * kmax * (1.0 + BOUND_SLACK) - dmin_ref[...] + SKIP_MARGIN
        cfirst = cfirst_ref[...]
        tile = lax.broadcasted_iota(jnp.int32, cfirst.shape, 0)
        count = jnp.zeros(cfirst.shape, jnp.int32)
        for j in range(o_ref.shape[0]):
            needed = (clast_ref[j:j + 1] - cfirst <= reach) & (tile > j)
            count = count + needed.astype(jnp.int32)
        o_ref[...] = count


def _fox_bounds(proj, c_sh, *, tq):
    s = proj.shape[0]
    nq = s // tq
    stat = pltpu.VMEM((nq, 8, LANES), F32)
    return pl.pallas_call(
        _bounds_kernel,
        grid=(nq,),
        in_specs=[
            pl.BlockSpec((tq, FOX_W), lambda i: (i, COL_Q // FOX_W)),
            pl.BlockSpec((tq, FOX_W), lambda i: (i, COL_K // FOX_W)),
            pl.BlockSpec((tq, LANES), lambda i: (i, 0)),
        ],
        out_specs=pl.BlockSpec((nq, 8, LANES), lambda i: (0, 0, 0)),
        out_shape=jax.ShapeDtypeStruct((nq, 8, LANES), jnp.int32),
        scratch_shapes=[pltpu.VMEM((FOX_W, LANES), BF16), stat, stat, stat, stat, stat],
        compiler_params=_params("arbitrary"),
        name="fox_bounds",
    )(proj, proj, c_sh)


def _mem_kernel(mem_ref, mg_ref, wkv_ref, q_ref, gate_ref, o_ref, kv_ref):
    i = pl.program_id(0)

    @pl.when(i == 0)
    def _():
        mem = mem_ref[...]
        ms = jnp.mean(mem * mem, axis=-1, keepdims=True)
        hm = (mem * lax.rsqrt(ms + EPS) * mg_ref[...]).astype(BF16)
        kv_ref[...] = jnp.dot(hm, wkv_ref[...], preferred_element_type=F32).astype(BF16)

    scale = HEAD_DIM ** -0.5
    for hh in range(N_MEM_HEADS):
        lo = hh * HEAD_DIM
        q = q_ref[:, lo:lo + HEAD_DIM]
        k = kv_ref[:, lo:lo + HEAD_DIM]
        v = kv_ref[:, MEM_W + lo:MEM_W + lo + HEAD_DIM]
        s = lax.dot_general(q, k, (((1,), (1,)), ((), ())), preferred_element_type=F32) * scale
        m = jnp.max(s, axis=1, keepdims=True)
        p = jnp.exp(s - m)
        l = jnp.sum(p, axis=1, keepdims=True)
        pv = jnp.dot(p.astype(BF16), v, preferred_element_type=F32)
        gate = gate_ref[:, lo:lo + HEAD_DIM].astype(F32)
        o_ref[:, lo:lo + HEAD_DIM] = (pv / l * _silu(gate)).astype(BF16)


def _mem_attention(proj, mem, mem_norm_g, w_kv, *, tm):
    s = proj.shape[0]
    m_len, d = mem.shape
    col = lambda c: c // MEM_W
    full = lambda shape: pl.BlockSpec(shape, lambda i: (0, 0))
    return pl.pallas_call(
        _mem_kernel,
        grid=(s // tm,),
        in_specs=[
            full((m_len, d)),
            full((1, d)),
            full((d, 2 * MEM_W)),
            pl.BlockSpec((tm, MEM_W), lambda i: (i, col(COL_MQ))),
            pl.BlockSpec((tm, MEM_W), lambda i: (i, col(COL_MEM_GATE))),
        ],
        out_specs=pl.BlockSpec((tm, MEM_W), lambda i: (i, 0)),
        out_shape=jax.ShapeDtypeStruct((s, MEM_W), BF16),
        scratch_shapes=[pltpu.VMEM((m_len, 2 * MEM_W), BF16)],
        compiler_params=_params("arbitrary"),
        name="mem_attention",
    )(mem, mem_norm_g, w_kv, proj, proj)


def _out_kernel(x_ref, yc_ref, yf_ref, ym_ref, w_ref, g_ref, o_ref):
    acc = jnp.dot(yc_ref[...], w_ref[0:CONV_W, :], preferred_element_type=F32)
    acc = acc + jnp.dot(yf_ref[...], w_ref[CONV_W:CONV_W + FOX_W, :], preferred_element_type=F32)
    acc = acc + jnp.dot(ym_ref[...], w_ref[CONV_W + FOX_W:, :], preferred_element_type=F32)
    r = x_ref[...] + acc
    ms = jnp.mean(r * r, axis=-1, keepdims=True)
    o_ref[...] = r * lax.rsqrt(ms + EPS) * g_ref[...]


def _out_proj(x, y_conv, y_fox, y_mem, w_out, final_g, *, tm):
    s, d = x.shape
    row = lambda w: pl.BlockSpec((tm, w), lambda i: (i, 0))
    return pl.pallas_call(
        _out_kernel,
        grid=(s // tm,),
        in_specs=[
            row(d), row(CONV_W), row(FOX_W), row(MEM_W),
            pl.BlockSpec(w_out.shape, lambda i: (0, 0)),
            pl.BlockSpec((1, d), lambda i: (0, 0)),
        ],
        out_specs=row(d),
        out_shape=jax.ShapeDtypeStruct((s, d), F32),
        compiler_params=_params("arbitrary"),
        name="out_proj",
    )(x, y_conv, y_fox, y_mem, w_out, final_g)


def _layer(x, mem, norm_g, mem_norm_g, w_in, b_f, conv_w, conv_b, ln_g, ln_b, w_pw, w_kv, w_out):
    d = x.shape[-1]
    w_main, w_f = _wprep(w_in.T)
    col_scale = jnp.ones((1, N_MAIN), F32).at[:, COL_Q:COL_K].set(HEAD_DIM ** -0.5 * LOG2E)
    b_f_row = jnp.pad(b_f, (0, LANES - N_FOX_HEADS)).reshape(1, LANES)

    proj, flog = _in_proj(x, norm_g.reshape(1, d), w_main, w_f, col_scale, tm=1024, tn=1664)
    c_hs, c_sh = _decay(flog, b_f_row)
    y_conv = _conv_branch(proj, conv_w, conv_b.reshape(1, -1), ln_g.reshape(1, -1), ln_b.reshape(1, -1),
                          w_pw.astype(BF16), tm=512)
    n_off = _fox_bounds(proj, c_sh, tq=FOX_TILE)[:, 0, :N_FOX_HEADS].T
    y_fox = _fox_attention(n_off, proj, c_hs.reshape(N_FOX_HEADS, 1, -1), tq=FOX_TILE, heads=4)
    y_mem = _mem_attention(proj, mem, mem_norm_g.reshape(1, d), w_kv.astype(BF16), tm=1024)
    return y_conv, y_fox, y_mem


def kernel(x, mem, norm_g, mem_norm_g, w_in, b_f, conv_w, conv_b, conv_ln_g, conv_ln_b, w_conv_pw,
           w_mem_kv, w_out, final_g):
    b, s, d = x.shape
    assert w_in.shape[0] == 1, "a single trunk layer is supported"
    outs = []
    for bi in range(b):
        y_conv, y_fox, y_mem = _layer(
            x[bi], mem[bi], norm_g[0], mem_norm_g[0], w_in[0], b_f[0], conv_w[0], conv_b[0],
            conv_ln_g[0], conv_ln_b[0], w_conv_pw[0], w_mem_kv[0], w_out[0])
        outs.append(_out_proj(x[bi], y_conv, y_fox, y_mem, w_out[0].astype(BF16),
                              final_g.reshape(1, d), tm=512))
    return jnp.stack(outs, axis=0)
```

```python
import functools

import jax
import jax.numpy as jnp
from jax import lax
from jax.experimental import pallas as pl
from jax.experimental.pallas import tpu as pltpu

F32 = jnp.float32
BF16 = jnp.bfloat16

LANES = 128
SUBLANES = 8
HEAD_DIM = 128
N_FOX_HEADS = 8
N_MEM_HEADS = 4
CONV_W = 512
FOX_W = N_FOX_HEADS * HEAD_DIM
MEM_W = N_MEM_HEADS * HEAD_DIM
CONV_K = 31
CONV_HALO = 32
EPS = 1e-6
MASK_VALUE = -1e30
LOG2E = 1.4426950408889634

COL_Q = 0
COL_K = COL_Q + FOX_W
COL_V = COL_K + FOX_W
COL_FOX_GATE = COL_V + FOX_W
COL_CONV_A = COL_FOX_GATE + FOX_W
COL_CONV_B = COL_CONV_A + CONV_W
COL_CONV_GATE = COL_CONV_B + CONV_W
COL_MQ = COL_CONV_GATE + CONV_W
COL_MEM_GATE = COL_MQ + MEM_W
N_MAIN = COL_MEM_GATE + MEM_W

FOX_TILE = 512
SKIP_MARGIN = 160.0
BOUND_SLACK = 2.0 ** -7

VMEM_LIMIT_BYTES = 56 * 1024 * 1024


def _params(*sem):
    return pltpu.CompilerParams(dimension_semantics=sem, vmem_limit_bytes=VMEM_LIMIT_BYTES)


def _silu(x):
    return x * jax.nn.sigmoid(x)


REF_CONV_A = 0
REF_Q = 3 * CONV_W
REF_F_LOGIT = REF_Q + 3 * FOX_W
REF_FOX_GATE = REF_F_LOGIT + N_FOX_HEADS
REF_MQ = REF_FOX_GATE + FOX_W
WPREP_ROWS = 512


def _wprep_source_rows():
    segments = [(REF_Q, 3 * FOX_W), (REF_FOX_GATE, FOX_W), (REF_CONV_A, 3 * CONV_W), (REF_MQ, 2 * MEM_W)]
    rows = []
    for start, width in segments:
        rows += [start + r for r in range(0, width, WPREP_ROWS)]
    assert len(rows) * WPREP_ROWS == N_MAIN
    return rows


def _wprep_kernel(src_ref, w_ref, f_ref, wm_ref, wf_ref):
    del src_ref
    wm_ref[...] = w_ref[...].astype(BF16)

    @pl.when(pl.program_id(0) == 0)
    def _():
        row = lax.broadcasted_iota(jnp.int32, wf_ref.shape, 0)
        wf_ref[...] = jnp.where(row < N_FOX_HEADS, f_ref[...], 0.0).astype(BF16)


def _wprep(w_t):
    d = w_t.shape[1]
    src = jnp.asarray([r // SUBLANES for r in _wprep_source_rows()], jnp.int32)
    grid_spec = pltpu.PrefetchScalarGridSpec(
        num_scalar_prefetch=1,
        grid=(N_MAIN // WPREP_ROWS,),
        in_specs=[
            pl.BlockSpec((pl.Element(WPREP_ROWS), pl.Element(d)), lambda i, src: (src[i] * SUBLANES, 0)),
            pl.BlockSpec((pl.Element(LANES), pl.Element(d)), lambda i, src: (REF_F_LOGIT, 0)),
        ],
        out_specs=[
            pl.BlockSpec((WPREP_ROWS, d), lambda i, src: (i, 0)),
            pl.BlockSpec((LANES, d), lambda i, src: (0, 0)),
        ],
    )
    return pl.pallas_call(
        _wprep_kernel,
        grid_spec=grid_spec,
        out_shape=[
            jax.ShapeDtypeStruct((N_MAIN, d), BF16),
            jax.ShapeDtypeStruct((LANES, d), BF16),
        ],
        compiler_params=_params("arbitrary"),
        name="w_in_layout",
    )(src, w_t, w_t)


_NT = (((1,), (1,)), ((), ()))


def _in_proj_kernel(x_ref, g_ref, w_ref, wf_ref, cs_ref, o_ref, flog_ref, h_ref, *, row_chunk):
    n = pl.program_id(1)
    tm = x_ref.shape[0]

    @pl.when(n == 0)
    def _():
        def chunk(ci, carry):
            r = pl.multiple_of(ci * row_chunk, row_chunk)
            x = x_ref[pl.ds(r, row_chunk), :]
            ms = jnp.mean(x * x, axis=-1, keepdims=True)
            h = x * lax.rsqrt(ms + EPS) * g_ref[...]
            h_ref[pl.ds(r, row_chunk), :] = h.astype(BF16)
            return carry

        lax.fori_loop(0, tm // row_chunk, chunk, 0)
        flog_ref[...] = lax.dot_general(h_ref[...], wf_ref[...], _NT, preferred_element_type=F32)

    acc = lax.dot_general(h_ref[...], w_ref[...], _NT, preferred_element_type=F32)
    o_ref[...] = (acc * cs_ref[...]).astype(BF16)


def _in_proj(x, norm_g, w_main, w_f, col_scale, *, tm, tn):
    s, d = x.shape
    n_main = w_main.shape[0]
    return pl.pallas_call(
        functools.partial(_in_proj_kernel, row_chunk=128),
        grid=(s // tm, n_main // tn),
        in_specs=[
            pl.BlockSpec((tm, d), lambda i, n: (i, 0)),
            pl.BlockSpec((1, d), lambda i, n: (0, 0)),
            pl.BlockSpec((tn, d), lambda i, n: (n, 0)),
            pl.BlockSpec((LANES, d), lambda i, n: (0, 0)),
            pl.BlockSpec((1, tn), lambda i, n: (0, n)),
        ],
        out_specs=[
            pl.BlockSpec((tm, tn), lambda i, n: (i, n)),
            pl.BlockSpec((tm, LANES), lambda i, n: (i, 0)),
        ],
        out_shape=[
            jax.ShapeDtypeStruct((s, n_main), BF16),
            jax.ShapeDtypeStruct((s, LANES), F32),
        ],
        scratch_shapes=[pltpu.VMEM((tm, d), BF16)],
        compiler_params=_params("arbitrary", "arbitrary"),
        name="in_proj",
    )(x, norm_g, w_main, w_f, col_scale)


def _decay_kernel(flog_ref, bf_ref, c_hs_ref, c_sh_ref):
    s = flog_ref.shape[0]
    logf = jax.nn.log_sigmoid(flog_ref[...] + bf_ref[...])
    c = logf.T[:N_FOX_HEADS, :]
    pos = lax.broadcasted_iota(jnp.int32, c.shape, 1)
    shift = 1
    while shift < s:
        c = c + jnp.where(pos >= shift, pltpu.roll(c, shift, 1), 0.0)
        shift *= 2
    c = c * LOG2E
    c_hs_ref[...] = c
    c_sh_ref[...] = jnp.concatenate([c, jnp.zeros((LANES - N_FOX_HEADS, s), F32)], axis=0).T


def _decay(flog, b_f_row):
    s = flog.shape[0]
    return pl.pallas_call(
        _decay_kernel,
        out_shape=[
            jax.ShapeDtypeStruct((N_FOX_HEADS, s), F32),
            jax.ShapeDtypeStruct((s, LANES), F32),
        ],
        compiler_params=pltpu.CompilerParams(vmem_limit_bytes=VMEM_LIMIT_BYTES),
        name="decay_cumsum",
    )(flog, b_f_row)


def _conv_kernel(a_ref, b_ref, ap_ref, bp_ref, gate_ref, cw_ref, cb_ref, lg_ref, lb_ref, pw_ref,
                 o_ref, u_ref, c_ref, *, row_chunk):
    i = pl.program_id(0)
    tm = a_ref.shape[0]

    shifted_rows = tm + CONV_HALO - SUBLANES
    prev = ap_ref[...].astype(F32) * jax.nn.sigmoid(bp_ref[...].astype(F32))
    u_ref[0, 0:CONV_HALO, :] = jnp.where(i > 0, prev, 0.0)
    u_ref[0, CONV_HALO:CONV_HALO + tm, :] = a_ref[...].astype(F32) * jax.nn.sigmoid(b_ref[...].astype(F32))
    for p in range(1, SUBLANES):
        u_ref[p, 0:shifted_rows, :] = u_ref[0, p:p + shifted_rows, :]

    first_tap = CONV_HALO - (CONV_K - 1)

    def chunk(ci, carry):
        r = pl.multiple_of(ci * row_chunk, row_chunk)
        acc = jnp.zeros((row_chunk, CONV_W), F32) + cb_ref[...]
        for k in range(CONV_K):
            phase, base = (first_tap + k) % SUBLANES, (first_tap + k) // SUBLANES * SUBLANES
            tap = u_ref[phase, pl.ds(pl.multiple_of(r + base, SUBLANES), row_chunk), :]
            acc = acc + cw_ref[k:k + 1, :] * tap
        c_ref[pl.ds(r, row_chunk), :] = acc
        return carry

    lax.fori_loop(0, tm // row_chunk, chunk, 0)

    conv = c_ref[...]
    mu = jnp.mean(conv, axis=-1, keepdims=True)
    cen = conv - mu
    var = jnp.mean(cen * cen, axis=-1, keepdims=True)
    y = cen * lax.rsqrt(var + EPS) * lg_ref[...] + lb_ref[...]
    y = _silu(y).astype(BF16)
    pw = jnp.dot(y, pw_ref[...], preferred_element_type=F32)
    o_ref[...] = (pw * _silu(gate_ref[...].astype(F32))).astype(BF16)


def _conv_branch(proj, conv_w, conv_b, ln_g, ln_b, w_pw, *, tm):
    s = proj.shape[0]
    halo_blocks = tm // CONV_HALO
    col = lambda c: c // CONV_W

    def prev_map(c):
        return lambda i: (jnp.maximum(i * halo_blocks - 1, 0), col(c))

    full = lambda shape: pl.BlockSpec(shape, lambda i: (0, 0))
    return pl.pallas_call(
        functools.partial(_conv_kernel, row_chunk=64),
        grid=(s // tm,),
        in_specs=[
            pl.BlockSpec((tm, CONV_W), lambda i: (i, col(COL_CONV_A))),
            pl.BlockSpec((tm, CONV_W), lambda i: (i, col(COL_CONV_B))),
            pl.BlockSpec((CONV_HALO, CONV_W), prev_map(COL_CONV_A)),
            pl.BlockSpec((CONV_HALO, CONV_W), prev_map(COL_CONV_B)),
            pl.BlockSpec((tm, CONV_W), lambda i: (i, col(COL_CONV_GATE))),
            full((CONV_K, CONV_W)),
            full((1, CONV_W)),
            full((1, CONV_W)),
            full((1, CONV_W)),
            full((CONV_W, CONV_W)),
        ],
        out_specs=pl.BlockSpec((tm, CONV_W), lambda i: (i, 0)),
        out_shape=jax.ShapeDtypeStruct((s, CONV_W), BF16),
        scratch_shapes=[pltpu.VMEM((SUBLANES, CONV_HALO + tm, CONV_W), F32), pltpu.VMEM((tm, CONV_W), F32)],
        compiler_params=_params("arbitrary"),
        name="conv_branch",
    )(proj, proj, proj, proj, proj, conv_w, conv_b, ln_g, ln_b, w_pw)


def _fox_kernel(noff_ref, q_ref, k_ref, v_ref, crow_ref, gate_ref, o_ref, m_ref, l_ref, acc_ref, sa_ref, sb_ref,
                *, heads):
    hp = pl.program_id(0)
    i = pl.program_id(1)
    tq = q_ref.shape[0]
    tk = tq
    lane_reps = tk // LANES

    q_start = pl.multiple_of(i * tq, tq)
    cbase = [crow_ref[hp * heads + g, :, pl.ds(q_start, LANES)][:, 0:1] for g in range(heads)]

    m_ref[...] = jnp.full(m_ref.shape, MASK_VALUE, F32)
    l_ref[...] = jnp.zeros(l_ref.shape, F32)
    acc_ref[...] = jnp.zeros(acc_ref.shape, F32)

    def logits(j, s_ref, masked):
        start = pl.multiple_of(j * tk, tk)
        for g in range(heads):
            lo = g * HEAD_DIM
            q = q_ref[:, lo:lo + HEAD_DIM]
            k = k_ref[pl.ds(start, tk), lo:lo + HEAD_DIM]
            ck = crow_ref[hp * heads + g, :, pl.ds(start, tk)] - cbase[g]
            s = lax.dot_general(q, k, (((1,), (1,)), ((), ())), preferred_element_type=F32) - ck
            if masked:
                row = lax.broadcasted_iota(jnp.int32, s.shape, 0)
                colp = lax.broadcasted_iota(jnp.int32, s.shape, 1)
                s = jnp.where(colp <= row, s, MASK_VALUE)
            s_ref[g] = s

    def absorb(j, s_ref):
        start = pl.multiple_of(j * tk, tk)
        for g in range(heads):
            lo = g * HEAD_DIM
            v = v_ref[pl.ds(start, tk), lo:lo + HEAD_DIM]
            s = s_ref[g]
            m_old = m_ref[g]
            m_new = jnp.maximum(m_old, jnp.max(s, axis=1, keepdims=True))
            alpha = jnp.exp2(m_old - m_new)
            p = jnp.exp2(s - jnp.concatenate([m_new] * lane_reps, axis=1))
            l_ref[g] = alpha * l_ref[g] + jnp.sum(p, axis=1, keepdims=True)
            acc_ref[g] = alpha * acc_ref[g] + jnp.dot(p.astype(BF16), v, preferred_element_type=F32)
            m_ref[g] = m_new

    n_off = noff_ref[hp * heads, i]
    for g in range(1, heads):
        n_off = jnp.maximum(n_off, noff_ref[hp * heads + g, i])

    logits(i, sa_ref, True)

    def pair(jp, carry):
        j = i - 2 * jp
        logits(j - 1, sb_ref, False)
        absorb(j, sa_ref)
        logits(j - 2, sa_ref, False)
        absorb(j - 1, sb_ref)
        return carry

    n_pairs = n_off // 2
    lax.fori_loop(0, n_pairs, pair, 0)
    j_last = i - 2 * n_pairs

    @pl.when(n_off % 2 == 1)
    def _():
        logits(j_last - 1, sb_ref, False)
        absorb(j_last, sa_ref)
        absorb(j_last - 1, sb_ref)

    @pl.when(n_off % 2 == 0)
    def _():
        absorb(j_last, sa_ref)

    for g in range(heads):
        lo = g * HEAD_DIM
        gate = gate_ref[:, lo:lo + HEAD_DIM].astype(F32)
        o_ref[:, lo:lo + HEAD_DIM] = (acc_ref[g] / l_ref[g] * _silu(gate)).astype(BF16)


def _fox_attention(n_off, proj, c_hs, *, tq, heads):
    s = proj.shape[0]
    w = heads * HEAD_DIM
    gcol = lambda c: c // w
    grid_spec = pltpu.PrefetchScalarGridSpec(
        num_scalar_prefetch=1,
        grid=(N_FOX_HEADS // heads, s // tq),
        in_specs=[
            pl.BlockSpec((tq, w), lambda h, i, n: (i, gcol(COL_Q) + h)),
            pl.BlockSpec((s, w), lambda h, i, n: (0, gcol(COL_K) + h)),
            pl.BlockSpec((s, w), lambda h, i, n: (0, gcol(COL_V) + h)),
            pl.BlockSpec((N_FOX_HEADS, 1, s), lambda h, i, n: (0, 0, 0)),
            pl.BlockSpec((tq, w), lambda h, i, n: (i, gcol(COL_FOX_GATE) + h)),
        ],
        out_specs=pl.BlockSpec((tq, w), lambda h, i, n: (i, h)),
        scratch_shapes=[
            pltpu.VMEM((heads, tq, LANES), F32),
            pltpu.VMEM((heads, tq, LANES), F32),
            pltpu.VMEM((heads, tq, HEAD_DIM), F32),
            pltpu.VMEM((heads, tq, tq), F32),
            pltpu.VMEM((heads, tq, tq), F32),
        ],
    )
    return pl.pallas_call(
        functools.partial(_fox_kernel, heads=heads),
        grid_spec=grid_spec,
        out_shape=jax.ShapeDtypeStruct((s, FOX_W), BF16),
        compiler_params=_params("arbitrary", "arbitrary"),
        name="fox_attention",
    )(n_off, proj, proj, proj, c_hs, proj)


def _bounds_kernel(q_ref, k_ref, c_ref, o_ref, e_ref, qmax_ref, kmax_ref, dmin_ref, cfirst_ref, clast_ref):
    i = pl.program_id(0)
    nq = pl.num_programs(0)
    tq = q_ref.shape[0]

    @pl.when(i == 0)
    def _():
        col_head = lax.broadcasted_iota(jnp.int32, e_ref.shape, 0) // HEAD_DIM
        head = lax.broadcasted_iota(jnp.int32, e_ref.shape, 1)
        e_ref[...] = jnp.where(col_head == head, 1.0, 0.0).astype(BF16)

    q = q_ref[...]
    k = k_ref[...]
    e = e_ref[...]
    head_sum = lambda x: jnp.dot(x, e, preferred_element_type=F32)
    qn = jnp.sqrt(head_sum(q * q))
    kn = jnp.sqrt(head_sum(k * k))
    d = -(1.0 + BOUND_SLACK) * qn * kn
    rep = lambda x: jnp.broadcast_to(x, (8, LANES))
    qmax_ref[i] = rep(jnp.max(qn, axis=0, keepdims=True))
    kmax_ref[i] = rep(jnp.max(kn, axis=0, keepdims=True))
    dmin_ref[i] = rep(jnp.min(d, axis=0, keepdims=True))
    cfirst_ref[i] = rep(c_ref[0:1, :])
    clast_ref[i] = rep(c_ref[tq - 1:tq, :])

    @pl.when(i == nq - 1)
    def _():
        kmax = jnp.max(kmax_ref[...], axis=0, keepdims=True)
        reach = qmax_ref[...] * kmax * (1.0 + BOUND_SLACK) - dmin_ref[...] + SKIP_MARGIN
        cfirst = cfirst_ref[...]
        tile = lax.broadcasted_iota(jnp.int32, cfirst.shape, 0)
        count = jnp.zeros(cfirst.shape, jnp.int32)
        for j in range(o_ref.shape[0]):
            needed = (clast_ref[j:j + 1] - cfirst <= reach) & (tile > j)
            count = count + needed.astype(jnp.int32)
        o_ref[...] = count


def _fox_bounds(proj, c_sh, *, tq):
    s = proj.shape[0]
    nq = s // tq
    stat = pltpu.VMEM((nq, 8, LANES), F32)
    return pl.pallas_call(
        _bounds_kernel,
        grid=(nq,),
        in_specs=[
            pl.BlockSpec((tq, FOX_W), lambda i: (i, COL_Q // FOX_W)),
            pl.BlockSpec((tq, FOX_W), lambda i: (i, COL_K // FOX_W)),
            pl.BlockSpec((tq, LANES), lambda i: (i, 0)),
        ],
        out_specs=pl.BlockSpec((nq, 8, LANES), lambda i: (0, 0, 0)),
        out_shape=jax.ShapeDtypeStruct((nq, 8, LANES), jnp.int32),
        scratch_shapes=[pltpu.VMEM((FOX_W, LANES), BF16), stat, stat, stat, stat, stat],
        compiler_params=_params("arbitrary"),
        name="fox_bounds",
    )(proj, proj, c_sh)


def _mem_kernel(mem_ref, mg_ref, wkv_ref, q_ref, gate_ref, o_ref, kv_ref):
    i = pl.program_id(0)

    @pl.when(i == 0)
    def _():
        mem = mem_ref[...]
        ms = jnp.mean(mem * mem, axis=-1, keepdims=True)
        hm = (mem * lax.rsqrt(ms + EPS) * mg_ref[...]).astype(BF16)
        kv_ref[...] = jnp.dot(hm, wkv_ref[...], preferred_element_type=F32).astype(BF16)

    scale = HEAD_DIM ** -0.5
    for hh in range(N_MEM_HEADS):
        lo = hh * HEAD_DIM
        q = q_ref[:, lo:lo + HEAD_DIM]
        k = kv_ref[:, lo:lo + HEAD_DIM]
        v = kv_ref[:, MEM_W + lo:MEM_W + lo + HEAD_DIM]
        s = lax.dot_general(q, k, (((1,), (1,)), ((), ())), preferred_element_type=F32) * scale
        m = jnp.max(s, axis=1, keepdims=True)
        p = jnp.exp(s - m)
        l = jnp.sum(p, axis=1, keepdims=True)
        pv = jnp.dot(p.astype(BF16), v, preferred_element_type=F32)
        gate = gate_ref[:, lo:lo + HEAD_DIM].astype(F32)
        o_ref[:, lo:lo + HEAD_DIM] = (pv / l * _silu(gate)).astype(BF16)


def _mem_attention(proj, mem, mem_norm_g, w_kv, *, tm):
    s = proj.shape[0]
    m_len, d = mem.shape
    col = lambda c: c // MEM_W
    full = lambda shape: pl.BlockSpec(shape, lambda i: (0, 0))
    return pl.pallas_call(
        _mem_kernel,
        grid=(s // tm,),
        in_specs=[
            full((m_len, d)),
            full((1, d)),
            full((d, 2 * MEM_W)),
            pl.BlockSpec((tm, MEM_W), lambda i: (i, col(COL_MQ))),
            pl.BlockSpec((tm, MEM_W), lambda i: (i, col(COL_MEM_GATE))),
        ],
        out_specs=pl.BlockSpec((tm, MEM_W), lambda i: (i, 0)),
        out_shape=jax.ShapeDtypeStruct((s, MEM_W), BF16),
        scratch_shapes=[pltpu.VMEM((m_len, 2 * MEM_W), BF16)],
        compiler_params=_params("arbitrary"),
        name="mem_attention",
    )(mem, mem_norm_g, w_kv, proj, proj)


def _out_kernel(x_ref, yc_ref, yf_ref, ym_ref, w_ref, g_ref, o_ref):
    acc = jnp.dot(yc_ref[...], w_ref[0:CONV_W, :], preferred_element_type=F32)
    acc = acc + jnp.dot(yf_ref[...], w_ref[CONV_W:CONV_W + FOX_W, :], preferred_element_type=F32)
    acc = acc + jnp.dot(ym_ref[...], w_ref[CONV_W + FOX_W:, :], preferred_element_type=F32)
    r = x_ref[...] + acc
    ms = jnp.mean(r * r, axis=-1, keepdims=True)
    o_ref[...] = r * lax.rsqrt(ms + EPS) * g_ref[...]


def _out_proj(x, y_conv, y_fox, y_mem, w_out, final_g, *, tm):
    s, d = x.shape
    row = lambda w: pl.BlockSpec((tm, w), lambda i: (i, 0))
    return pl.pallas_call(
        _out_kernel,
        grid=(s // tm,),
        in_specs=[
            row(d), row(CONV_W), row(FOX_W), row(MEM_W),
            pl.BlockSpec(w_out.shape, lambda i: (0, 0)),
            pl.BlockSpec((1, d), lambda i: (0, 0)),
        ],
        out_specs=row(d),
        out_shape=jax.ShapeDtypeStruct((s, d), F32),
        compiler_params=_params("arbitrary"),
        name="out_proj",
    )(x, y_conv, y_fox, y_mem, w_out, final_g)


def _layer(x, mem, norm_g, mem_norm_g, w_in, b_f, conv_w, conv_b, ln_g, ln_b, w_pw, w_kv, w_out):
    d = x.shape[-1]
    w_main, w_f = _wprep(w_in.T)
    col_scale = jnp.ones((1, N_MAIN), F32).at[:, COL_Q:COL_K].set(HEAD_DIM ** -0.5 * LOG2E)
    b_f_row = jnp.pad(b_f, (0, LANES - N_FOX_HEADS)).reshape(1, LANES)

    proj, flog = _in_proj(x, norm_g.reshape(1, d), w_main, w_f, col_scale, tm=1024, tn=1664)
    c_hs, c_sh = _decay(flog, b_f_row)
    y_conv = _conv_branch(proj, conv_w, conv_b.reshape(1, -1), ln_g.reshape(1, -1), ln_b.reshape(1, -1),
                          w_pw.astype(BF16), tm=512)
    n_off = _fox_bounds(proj, c_sh, tq=FOX_TILE)[:, 0, :N_FOX_HEADS].T
    y_fox = _fox_attention(n_off, proj, c_hs.reshape(N_FOX_HEADS, 1, -1), tq=FOX_TILE, heads=4)
    y_mem = _mem_attention(proj, mem, mem_norm_g.reshape(1, d), w_kv.astype(BF16), tm=1024)
    return y_conv, y_fox, y_mem


def kernel(x, mem, norm_g, mem_norm_g, w_in, b_f, conv_w, conv_b, conv_ln_g, conv_ln_b, w_conv_pw,
           w_mem_kv, w_out, final_g):
    b, s, d = x.shape
    assert w_in.shape[0] == 1, "a single trunk layer is supported"
    outs = []
    for bi in range(b):
        y_conv, y_fox, y_mem = _layer(
            x[bi], mem[bi], norm_g[0], mem_norm_g[0], w_in[0], b_f[0], conv_w[0], conv_b[0],
            conv_ln_g[0], conv_ln_b[0], w_conv_pw[0], w_mem_kv[0], w_out[0])
        outs.append(_out_proj(x[bi], y_conv, y_fox, y_mem, w_out[0].astype(BF16),
                              final_g.reshape(1, d), tm=512))
    return jnp.stack(outs, axis=0)
```

```python
import functools

import jax
import jax.numpy as jnp
from jax import lax
from jax.experimental import pallas as pl
from jax.experimental.pallas import tpu as pltpu

F32 = jnp.float32
BF16 = jnp.bfloat16

LANES = 128
SUBLANES = 8
HEAD_DIM = 128
N_FOX_HEADS = 8
N_MEM_HEADS = 4
CONV_W = 512
FOX_W = N_FOX_HEADS * HEAD_DIM
MEM_W = N_MEM_HEADS * HEAD_DIM
CONV_K = 31
CONV_HALO = 32
EPS = 1e-6
MASK_VALUE = -1e30
LOG2E = 1.4426950408889634

COL_Q = 0
COL_K = COL_Q + FOX_W
COL_V = COL_K + FOX_W
COL_FOX_GATE = COL_V + FOX_W
COL_CONV_A = COL_FOX_GATE + FOX_W
COL_CONV_B = COL_CONV_A + CONV_W
COL_CONV_GATE = COL_CONV_B + CONV_W
COL_MQ = COL_CONV_GATE + CONV_W
COL_MEM_GATE = COL_MQ + MEM_W
N_MAIN = COL_MEM_GATE + MEM_W

FOX_TILE = 512
SKIP_MARGIN = 160.0
BOUND_SLACK = 2.0 ** -7

VMEM_LIMIT_BYTES = 56 * 1024 * 1024


def _params(*sem):
    return pltpu.CompilerParams(dimension_semantics=sem, vmem_limit_bytes=VMEM_LIMIT_BYTES)


def _silu(x):
    return x * jax.nn.sigmoid(x)


REF_CONV_A = 0
REF_Q = 3 * CONV_W
REF_F_LOGIT = REF_Q + 3 * FOX_W
REF_FOX_GATE = REF_F_LOGIT + N_FOX_HEADS
REF_MQ = REF_FOX_GATE + FOX_W
WPREP_ROWS = 512


def _wprep_source_rows():
    segments = [(REF_Q, 3 * FOX_W), (REF_FOX_GATE, FOX_W), (REF_CONV_A, 3 * CONV_W), (REF_MQ, 2 * MEM_W)]
    rows = []
    for start, width in segments:
        rows += [start + r for r in range(0, width, WPREP_ROWS)]
    assert len(rows) * WPREP_ROWS == N_MAIN
    return rows


def _wprep_kernel(src_ref, w_ref, wm_ref):
    del src_ref
    wm_ref[...] = w_ref[...].astype(BF16)


def _wprep(w_t):
    d = w_t.shape[1]
    src = jnp.asarray([r // SUBLANES for r in _wprep_source_rows()], jnp.int32)
    grid_spec = pltpu.PrefetchScalarGridSpec(
        num_scalar_prefetch=1,
        grid=(N_MAIN // WPREP_ROWS,),
        in_specs=[
            pl.BlockSpec((pl.Element(WPREP_ROWS), pl.Element(d)), lambda i, src: (src[i] * SUBLANES, 0)),
        ],
        out_specs=pl.BlockSpec((WPREP_ROWS, d), lambda i, src: (i, 0)),
    )
    return pl.pallas_call(
        _wprep_kernel,
        grid_spec=grid_spec,
        out_shape=jax.ShapeDtypeStruct((N_MAIN, d), BF16),
        compiler_params=_params("arbitrary"),
        name="w_in_layout",
    )(src, w_t)


_NT = (((1,), (1,)), ((), ()))


def _in_proj_kernel(x_ref, g_ref, w_ref, wf_ref, cs_ref, o_ref, flog_ref, h0_ref, h1_ref, *, row_chunk):
    i = pl.program_id(0)
    n = pl.program_id(1)
    n_tiles = pl.num_programs(0) - 1
    tm = x_ref.shape[0]
    rows = tm // pl.num_programs(1)
    r0 = pl.multiple_of(n * rows, rows)

    def normalise(h_ref):
        for c in range(rows // row_chunk):
            r = pl.multiple_of(r0 + c * row_chunk, row_chunk)
            x = x_ref[pl.ds(r, row_chunk), :]
            ms = jnp.mean(x * x, axis=-1, keepdims=True)
            h = x * lax.rsqrt(ms + EPS) * g_ref[...]
            h_ref[pl.ds(r, row_chunk), :] = h.astype(BF16)

    def project(h_ref):
        acc = lax.dot_general(h_ref[...], w_ref[...], _NT, preferred_element_type=F32)
        o_ref[...] = (acc * cs_ref[...]).astype(BF16)
        row = lax.broadcasted_iota(jnp.int32, wf_ref.shape, 0)
        wf = jnp.where(row < N_FOX_HEADS, wf_ref[...], 0.0).astype(BF16)
        flog_ref[pl.ds(r0, rows), :] = lax.dot_general(
            h_ref[pl.ds(r0, rows), :], wf, _NT, preferred_element_type=F32)

    middle = (i > 0) & (i < n_tiles)

    @pl.when(i == 0)
    def _():
        normalise(h0_ref)

    @pl.when(middle & (i % 2 == 1))
    def _():
        project(h0_ref)
        normalise(h1_ref)

    @pl.when(middle & (i % 2 == 0))
    def _():
        project(h1_ref)
        normalise(h0_ref)

    @pl.when(i == n_tiles)
    def _():
        project(h1_ref)


def _in_proj(x, norm_g, w_main, w_t, col_scale, *, tm, tn):
    s, d = x.shape
    n_main = w_main.shape[0]
    n_tiles = s // tm
    assert n_tiles % 2 == 0
    prev = lambda i: jnp.maximum(i - 1, 0)
    col = lambda i, n: jnp.where(i == 0, 0, n)
    return pl.pallas_call(
        functools.partial(_in_proj_kernel, row_chunk=128),
        grid=(n_tiles + 1, n_main // tn),
        in_specs=[
            pl.BlockSpec((tm, d), lambda i, n: (jnp.minimum(i, n_tiles - 1), 0)),
            pl.BlockSpec((1, d), lambda i, n: (0, 0)),
            pl.BlockSpec((tn, d), lambda i, n: (col(i, n), 0)),
            pl.BlockSpec((pl.Element(LANES), pl.Element(d)), lambda i, n: (REF_F_LOGIT, 0)),
            pl.BlockSpec((1, tn), lambda i, n: (0, col(i, n))),
        ],
        out_specs=[
            pl.BlockSpec((tm, tn), lambda i, n: (prev(i), col(i, n))),
            pl.BlockSpec((tm, LANES), lambda i, n: (prev(i), 0)),
        ],
        out_shape=[
            jax.ShapeDtypeStruct((s, n_main), BF16),
            jax.ShapeDtypeStruct((s, LANES), F32),
        ],
        scratch_shapes=[pltpu.VMEM((tm, d), BF16), pltpu.VMEM((tm, d), BF16)],
        compiler_params=_params("arbitrary", "arbitrary"),
        name="in_proj",
    )(x, norm_g, w_main, w_t, col_scale)


def _decay_kernel(flog_ref, bf_ref, c_hs_ref, c_sh_ref):
    s = flog_ref.shape[0]
    logf = jax.nn.log_sigmoid(flog_ref[...] + bf_ref[...])
    c = logf.T[:N_FOX_HEADS, :]
    pos = lax.broadcasted_iota(jnp.int32, c.shape, 1)
    shift = 1
    while shift < s:
        c = c + jnp.where(pos >= shift, pltpu.roll(c, shift, 1), 0.0)
        shift *= 2
    c = c * LOG2E
    c_hs_ref[...] = c
    c_sh_ref[...] = jnp.concatenate([c, jnp.zeros((LANES - N_FOX_HEADS, s), F32)], axis=0).T


def _decay(flog, b_f_row):
    s = flog.shape[0]
    return pl.pallas_call(
        _decay_kernel,
        out_shape=[
            jax.ShapeDtypeStruct((N_FOX_HEADS, s), F32),
            jax.ShapeDtypeStruct((s, LANES), F32),
        ],
        compiler_params=pltpu.CompilerParams(vmem_limit_bytes=VMEM_LIMIT_BYTES),
        name="decay_cumsum",
    )(flog, b_f_row)


def _conv_kernel(a_ref, b_ref, ap_ref, bp_ref, gate_ref, cw_ref, cb_ref, lg_ref, lb_ref, pw_ref,
                 o_ref, u_ref, c_ref, *, row_chunk):
    i = pl.program_id(0)
    tm = a_ref.shape[0]

    shifted_rows = tm + CONV_HALO - SUBLANES
    prev = jnp.where(i > 0, ap_ref[...].astype(F32) * jax.nn.sigmoid(bp_ref[...].astype(F32)), 0.0)
    cur = a_ref[...].astype(F32) * jax.nn.sigmoid(b_ref[...].astype(F32))
    for g in range(CONV_W // LANES):
        lo = g * LANES
        u_ref[0, g, 0:CONV_HALO, :] = prev[:, lo:lo + LANES]
        u_ref[0, g, CONV_HALO:CONV_HALO + tm, :] = cur[:, lo:lo + LANES]
        for p in range(1, SUBLANES):
            u_ref[p, g, 0:shifted_rows, :] = u_ref[0, g, p:p + shifted_rows, :]

    first_tap = CONV_HALO - (CONV_K - 1)

    for g in range(CONV_W // LANES):
        lo = g * LANES
        taps_w = [jnp.broadcast_to(cw_ref[k:k + 1, lo:lo + LANES], (SUBLANES, LANES)) for k in range(CONV_K)]
        bias = cb_ref[:, lo:lo + LANES]

        def chunk(ci, carry, g=g, lo=lo, taps_w=taps_w, bias=bias):
            r = pl.multiple_of(ci * row_chunk, row_chunk)
            acc = jnp.zeros((row_chunk, LANES), F32) + bias
            for k in range(CONV_K):
                phase, base = (first_tap + k) % SUBLANES, (first_tap + k) // SUBLANES * SUBLANES
                tap = u_ref[phase, g, pl.ds(pl.multiple_of(r + base, SUBLANES), row_chunk), :]
                w = jnp.concatenate([taps_w[k]] * (row_chunk // SUBLANES), axis=0)
                acc = acc + w * tap
            c_ref[pl.ds(r, row_chunk), lo:lo + LANES] = acc
            return carry

        lax.fori_loop(0, tm // row_chunk, chunk, 0)

    conv = c_ref[...]
    mu = jnp.mean(conv, axis=-1, keepdims=True)
    cen = conv - mu
    var = jnp.mean(cen * cen, axis=-1, keepdims=True)
    y = cen * lax.rsqrt(var + EPS) * lg_ref[...] + lb_ref[...]
    y = _silu(y).astype(BF16)
    pw = jnp.dot(y, pw_ref[...], preferred_element_type=F32)
    o_ref[...] = (pw * _silu(gate_ref[...].astype(F32))).astype(BF16)


def _conv_branch(proj, conv_w, conv_b, ln_g, ln_b, w_pw, *, tm):
    s = proj.shape[0]
    halo_blocks = tm // CONV_HALO
    col = lambda c: c // CONV_W

    def prev_map(c):
        return lambda i: (jnp.maximum(i * halo_blocks - 1, 0), col(c))

    full = lambda shape: pl.BlockSpec(shape, lambda i: (0, 0))
    return pl.pallas_call(
        functools.partial(_conv_kernel, row_chunk=128),
        grid=(s // tm,),
        in_specs=[
            pl.BlockSpec((tm, CONV_W), lambda i: (i, col(COL_CONV_A))),
            pl.BlockSpec((tm, CONV_W), lambda i: (i, col(COL_CONV_B))),
            pl.BlockSpec((CONV_HALO, CONV_W), prev_map(COL_CONV_A)),
            pl.BlockSpec((CONV_HALO, CONV_W), prev_map(COL_CONV_B)),
            pl.BlockSpec((tm, CONV_W), lambda i: (i, col(COL_CONV_GATE))),
            full((CONV_K, CONV_W)),
            full((1, CONV_W)),
            full((1, CONV_W)),
            full((1, CONV_W)),
            full((CONV_W, CONV_W)),
        ],
        out_specs=pl.BlockSpec((tm, CONV_W), lambda i: (i, 0)),
        out_shape=jax.ShapeDtypeStruct((s, CONV_W), BF16),
        scratch_shapes=[pltpu.VMEM((SUBLANES, CONV_W // LANES, CONV_HALO + tm, LANES), F32),
                        pltpu.VMEM((tm, CONV_W), F32)],
        compiler_params=_params("arbitrary"),
        name="conv_branch",
    )(proj, proj, proj, proj, proj, conv_w, conv_b, ln_g, ln_b, w_pw)


def _fox_kernel(noff_ref, q_ref, k_ref, v_ref, crow_ref, gate_ref, o_ref, m_ref, l_ref, acc_ref, sa_ref, sb_ref,
                *, heads):
    hp = pl.program_id(0)
    i = pl.program_id(1)
    tq = q_ref.shape[0]
    tk = tq
    lane_reps = tk // LANES

    q_start = pl.multiple_of(i * tq, tq)
    cbase = [crow_ref[hp * heads + g, :, pl.ds(q_start, LANES)][:, 0:1] for g in range(heads)]

    m_ref[...] = jnp.full(m_ref.shape, MASK_VALUE, F32)
    l_ref[...] = jnp.zeros(l_ref.shape, F32)
    acc_ref[...] = jnp.zeros(acc_ref.shape, F32)

    def logits(j, s_ref, masked):
        start = pl.multiple_of(j * tk, tk)
        for g in range(heads):
            lo = g * HEAD_DIM
            q = q_ref[:, lo:lo + HEAD_DIM]
            k = k_ref[pl.ds(start, tk), lo:lo + HEAD_DIM]
            ck = crow_ref[hp * heads + g, :, pl.ds(start, tk)] - cbase[g]
            s = lax.dot_general(q, k, (((1,), (1,)), ((), ())), preferred_element_type=F32) - ck
            if masked:
                row = lax.broadcasted_iota(jnp.int32, s.shape, 0)
                colp = lax.broadcasted_iota(jnp.int32, s.shape, 1)
                s = jnp.where(colp <= row, s, MASK_VALUE)
            s_ref[g] = s

    def absorb(j, s_ref):
        start = pl.multiple_of(j * tk, tk)
        for g in range(heads):
            lo = g * HEAD_DIM
            v = v_ref[pl.ds(start, tk), lo:lo + HEAD_DIM]
            s = s_ref[g]
            m_old = m_ref[g]
            m_new = jnp.maximum(m_old, jnp.max(s, axis=1, keepdims=True))
            alpha = jnp.exp2(m_old - m_new)
            p = jnp.exp2(s - jnp.concatenate([m_new] * lane_reps, axis=1))
            l_ref[g] = alpha * l_ref[g] + jnp.sum(p, axis=1, keepdims=True)
            acc_ref[g] = alpha * acc_ref[g] + jnp.dot(p.astype(BF16), v, preferred_element_type=F32)
            m_ref[g] = m_new

    n_off = noff_ref[hp * heads, i]
    for g in range(1, heads):
        n_off = jnp.maximum(n_off, noff_ref[hp * heads + g, i])

    logits(i, sa_ref, True)

    def pair(jp, carry):
        j = i - 2 * jp
        logits(j - 1, sb_ref, False)
        absorb(j, sa_ref)
        logits(j - 2, sa_ref, False)
        absorb(j - 1, sb_ref)
        return carry

    n_pairs = n_off // 2
    lax.fori_loop(0, n_pairs, pair, 0)
    j_last = i - 2 * n_pairs

    @pl.when(n_off % 2 == 1)
    def _():
        logits(j_last - 1, sb_ref, False)
        absorb(j_last, sa_ref)
        absorb(j_last - 1, sb_ref)

    @pl.when(n_off % 2 == 0)
    def _():
        absorb(j_last, sa_ref)

    for g in range(heads):
        lo = g * HEAD_DIM
        gate = gate_ref[:, lo:lo + HEAD_DIM].astype(F32)
        o_ref[:, lo:lo + HEAD_DIM] = (acc_ref[g] / l_ref[g] * _silu(gate)).astype(BF16)


def _fox_attention(n_off, proj, c_hs, *, tq, heads):
    s = proj.shape[0]
    w = heads * HEAD_DIM
    gcol = lambda c: c // w
    grid_spec = pltpu.PrefetchScalarGridSpec(
        num_scalar_prefetch=1,
        grid=(N_FOX_HEADS // heads, s // tq),
        in_specs=[
            pl.BlockSpec((tq, w), lambda h, i, n: (i, gcol(COL_Q) + h)),
            pl.BlockSpec((s, w), lambda h, i, n: (0, gcol(COL_K) + h)),
            pl.BlockSpec((s, w), lambda h, i, n: (0, gcol(COL_V) + h)),
            pl.BlockSpec((N_FOX_HEADS, 1, s), lambda h, i, n: (0, 0, 0)),
            pl.BlockSpec((tq, w), lambda h, i, n: (i, gcol(COL_FOX_GATE) + h)),
        ],
        out_specs=pl.BlockSpec((tq, w), lambda h, i, n: (i, h)),
        scratch_shapes=[
            pltpu.VMEM((heads, tq, LANES), F32),
            pltpu.VMEM((heads, tq, LANES), F32),
            pltpu.VMEM((heads, tq, HEAD_DIM), F32),
            pltpu.VMEM((heads, tq, tq), F32),
            pltpu.VMEM((heads, tq, tq), F32),
        ],
    )
    return pl.pallas_call(
        functools.partial(_fox_kernel, heads=heads),
        grid_spec=grid_spec,
        out_shape=jax.ShapeDtypeStruct((s, FOX_W), BF16),
        compiler_params=_params("arbitrary", "arbitrary"),
        name="fox_attention",
    )(n_off, proj, proj, proj, c_hs, proj)


def _bounds_kernel(q_ref, k_ref, c_ref, o_ref, e_ref, qmax_ref, kmax_ref, dmin_ref, cfirst_ref, clast_ref):
    i = pl.program_id(0)
    nq = pl.num_programs(0)
    tq = q_ref.shape[0]

    @pl.when(i == 0)
    def _():
        col_head = lax.broadcasted_iota(jnp.int32, e_ref.shape, 0) // HEAD_DIM
        head = lax.broadcasted_iota(jnp.int32, e_ref.shape, 1)
        e_ref[...] = jnp.where(col_head == head, 1.0, 0.0).astype(BF16)

    q = q_ref[...]
    k = k_ref[...]
    e = e_ref[...]
    head_sum = lambda x: jnp.dot(x, e, preferred_element_type=F32)
    qn = jnp.sqrt(head_sum(q * q))
    kn = jnp.sqrt(head_sum(k * k))
    d = -(1.0 + BOUND_SLACK) * qn * kn
    rep = lambda x: jnp.broadcast_to(x, (8, LANES))
    qmax_ref[i] = rep(jnp.max(qn, axis=0, keepdims=True))
    kmax_ref[i] = rep(jnp.max(kn, axis=0, keepdims=True))
    dmin_ref[i] = rep(jnp.min(d, axis=0, keepdims=True))
    cfirst_ref[i] = rep(c_ref[0:1, :])
    clast_ref[i] = rep(c_ref[tq - 1:tq, :])

    @pl.when(i == nq - 1)
    def _():
        kmax = jnp.max(kmax_ref[...], axis=0, keepdims=True)
        reach = qmax_ref[...] * kmax * (1.0 + BOUND_SLACK) - dmin_ref[...] + SKIP_MARGIN
        cfirst = cfirst_ref[...]
        tile = lax.broadcasted_iota(jnp.int32, cfirst.shape, 0)
        count = jnp.zeros(cfirst.shape, jnp.int32)
        for j in range(o_ref.shape[0]):
            needed = (clast_ref[j:j + 1] - cfirst <= reach) & (tile > j)
            count = count + needed.astype(jnp.int32)
        o_ref[...] = count


def _fox_bounds(proj, c_sh, *, tq):
    s = proj.shape[0]
    nq = s // tq
    stat = pltpu.VMEM((nq, 8, LANES), F32)
    return pl.pallas_call(
        _bounds_kernel,
        grid=(nq,),
        in_specs=[
            pl.BlockSpec((tq, FOX_W), lambda i: (i, COL_Q // FOX_W)),
            pl.BlockSpec((tq, FOX_W), lambda i: (i, COL_K // FOX_W)),
            pl.BlockSpec((tq, LANES), lambda i: (i, 0)),
        ],
        out_specs=pl.BlockSpec((nq, 8, LANES), lambda i: (0, 0, 0)),
        out_shape=jax.ShapeDtypeStruct((nq, 8, LANES), jnp.int32),
        scratch_shapes=[pltpu.VMEM((FOX_W, LANES), BF16), stat, stat, stat, stat, stat],
        compiler_params=_params("arbitrary"),
        name="fox_bounds",
    )(proj, proj, c_sh)


def _mem_kernel(mem_ref, mg_ref, wkv_ref, q_ref, gate_ref, o_ref, kv_ref):
    i = pl.program_id(0)

    @pl.when(i == 0)
    def _():
        mem = mem_ref[...]
        ms = jnp.mean(mem * mem, axis=-1, keepdims=True)
        hm = (mem * lax.rsqrt(ms + EPS) * mg_ref[...]).astype(BF16)
        kv_ref[...] = jnp.dot(hm, wkv_ref[...], preferred_element_type=F32).astype(BF16)

    scale = HEAD_DIM ** -0.5
    for hh in range(N_MEM_HEADS):
        lo = hh * HEAD_DIM
        q = q_ref[:, lo:lo + HEAD_DIM]
        k = kv_ref[:, lo:lo + HEAD_DIM]
        v = kv_ref[:, MEM_W + lo:MEM_W + lo + HEAD_DIM]
        s = lax.dot_general(q, k, (((1,), (1,)), ((), ())), preferred_element_type=F32) * scale
        m = jnp.max(s, axis=1, keepdims=True)
        p = jnp.exp(s - m)
        l = jnp.sum(p, axis=1, keepdims=True)
        pv = jnp.dot(p.astype(BF16), v, preferred_element_type=F32)
        gate = gate_ref[:, lo:lo + HEAD_DIM].astype(F32)
        o_ref[:, lo:lo + HEAD_DIM] = (pv / l * _silu(gate)).astype(BF16)


def _mem_attention(proj, mem, mem_norm_g, w_kv, *, tm):
    s = proj.shape[0]
    m_len, d = mem.shape
    col = lambda c: c // MEM_W
    full = lambda shape: pl.BlockSpec(shape, lambda i: (0, 0))
    return pl.pallas_call(
        _mem_kernel,
        grid=(s // tm,),
        in_specs=[
            full((m_len, d)),
            full((1, d)),
            full((d, 2 * MEM_W)),
            pl.BlockSpec((tm, MEM_W), lambda i: (i, col(COL_MQ))),
            pl.BlockSpec((tm, MEM_W), lambda i: (i, col(COL_MEM_GATE))),
        ],
        out_specs=pl.BlockSpec((tm, MEM_W), lambda i: (i, 0)),
        out_shape=jax.ShapeDtypeStruct((s, MEM_W), BF16),
        scratch_shapes=[pltpu.VMEM((m_len, 2 * MEM_W), BF16)],
        compiler_params=_params("arbitrary"),
        name="mem_attention",
    )(mem, mem_norm_g, w_kv, proj, proj)


def _out_kernel(x_ref, yc_ref, yf_ref, ym_ref, w_ref, g_ref, o_ref):
    acc = jnp.dot(yc_ref[...], w_ref[0:CONV_W, :], preferred_element_type=F32)
    acc = acc + jnp.dot(yf_ref[...], w_ref[CONV_W:CONV_W + FOX_W, :], preferred_element_type=F32)
    acc = acc + jnp.dot(ym_ref[...], w_ref[CONV_W + FOX_W:, :], preferred_element_type=F32)
    r = x_ref[...] + acc
    ms = jnp.mean(r * r, axis=-1, keepdims=True)
    o_ref[...] = r * lax.rsqrt(ms + EPS) * g_ref[...]


def _out_proj(x, y_conv, y_fox, y_mem, w_out, final_g, *, tm):
    s, d = x.shape
    row = lambda w: pl.BlockSpec((tm, w), lambda i: (i, 0))
    return pl.pallas_call(
        _out_kernel,
        grid=(s // tm,),
        in_specs=[
            row(d), row(CONV_W), row(FOX_W), row(MEM_W),
            pl.BlockSpec(w_out.shape, lambda i: (0, 0)),
            pl.BlockSpec((1, d), lambda i: (0, 0)),
        ],
        out_specs=row(d),
        out_shape=jax.ShapeDtypeStruct((s, d), F32),
        compiler_params=_params("arbitrary"),
        name="out_proj",
    )(x, y_conv, y_fox, y_mem, w_out, final_g)


def _layer(x, mem, norm_g, mem_norm_g, w_in, b_f, conv_w, conv_b, ln_g, ln_b, w_pw, w_kv, w_out):
    d = x.shape[-1]
    w_t = w_in.T
    w_main = _wprep(w_t)
    col_scale = jnp.ones((1, N_MAIN), F32).at[:, COL_Q:COL_K].set(HEAD_DIM ** -0.5 * LOG2E)
    b_f_row = jnp.pad(b_f, (0, LANES - N_FOX_HEADS)).reshape(1, LANES)

    proj, flog = _in_proj(x, norm_g.reshape(1, d), w_main, w_t, col_scale, tm=1024, tn=1664)
    c_hs, c_sh = _decay(flog, b_f_row)
    y_conv = _conv_branch(proj, conv_w, conv_b.reshape(1, -1), ln_g.reshape(1, -1), ln_b.reshape(1, -1),
                          w_pw.astype(BF16), tm=512)
    n_off = _fox_bounds(proj, c_sh, tq=FOX_TILE)[:, 0, :N_FOX_HEADS].T
    y_fox = _fox_attention(n_off, proj, c_hs.reshape(N_FOX_HEADS, 1, -1), tq=FOX_TILE, heads=4)
    y_mem = _mem_attention(proj, mem, mem_norm_g.reshape(1, d), w_kv.astype(BF16), tm=1024)
    return y_conv, y_fox, y_mem


def kernel(x, mem, norm_g, mem_norm_g, w_in, b_f, conv_w, conv_b, conv_ln_g, conv_ln_b, w_conv_pw,
           w_mem_kv, w_out, final_g):
    b, s, d = x.shape
    assert w_in.shape[0] == 1, "a single trunk layer is supported"
    outs = []
    for bi in range(b):
        y_conv, y_fox, y_mem = _layer(
            x[bi], mem[bi], norm_g[0], mem_norm_g[0], w_in[0], b_f[0], conv_w[0], conv_b[0],
            conv_ln_g[0], conv_ln_b[0], w_conv_pw[0], w_mem_kv[0], w_out[0])
        outs.append(_out_proj(x[bi], y_conv, y_fox, y_mem, w_out[0].astype(BF16),
                              final_g.reshape(1, d), tm=512))
    return jnp.stack(outs, axis=0)
```

```python
import functools

import jax
import jax.numpy as jnp
from jax import lax
from jax.experimental import pallas as pl
from jax.experimental.pallas import tpu as pltpu

F32 = jnp.float32
BF16 = jnp.bfloat16

LANES = 128
SUBLANES = 8
HEAD_DIM = 128
N_FOX_HEADS = 8
N_MEM_HEADS = 4
CONV_W = 512
FOX_W = N_FOX_HEADS * HEAD_DIM
MEM_W = N_MEM_HEADS * HEAD_DIM
CONV_K = 31
CONV_HALO = 32
EPS = 1e-6
MASK_VALUE = -1e30
LOG2E = 1.4426950408889634

COL_Q = 0
COL_K = COL_Q + FOX_W
COL_V = COL_K + FOX_W
COL_FOX_GATE = COL_V + FOX_W
COL_CONV_A = COL_FOX_GATE + FOX_W
COL_CONV_B = COL_CONV_A + CONV_W
COL_CONV_GATE = COL_CONV_B + CONV_W
COL_MQ = COL_CONV_GATE + CONV_W
COL_MEM_GATE = COL_MQ + MEM_W
N_MAIN = COL_MEM_GATE + MEM_W

FOX_TQ = 256
FOX_TK = 512
SKIP_MARGIN = 152.0
BOUND_SLACK = 2.0 ** -7

VMEM_LIMIT_BYTES = 56 * 1024 * 1024


def _params(*sem):
    return pltpu.CompilerParams(dimension_semantics=sem, vmem_limit_bytes=VMEM_LIMIT_BYTES)


def _silu(x):
    return x * jax.nn.sigmoid(x)


REF_CONV_A = 0
REF_Q = 3 * CONV_W
REF_F_LOGIT = REF_Q + 3 * FOX_W
REF_FOX_GATE = REF_F_LOGIT + N_FOX_HEADS
REF_MQ = REF_FOX_GATE + FOX_W
WPREP_ROWS = 512


def _wprep_source_rows():
    segments = [(REF_Q, 3 * FOX_W), (REF_FOX_GATE, FOX_W), (REF_CONV_A, 3 * CONV_W), (REF_MQ, 2 * MEM_W)]
    rows = []
    for start, width in segments:
        rows += [start + r for r in range(0, width, WPREP_ROWS)]
    assert len(rows) * WPREP_ROWS == N_MAIN
    return rows


def _wprep_kernel(src_ref, w_ref, wm_ref):
    del src_ref
    wm_ref[...] = w_ref[...].astype(BF16)


def _wprep(w_t):
    d = w_t.shape[1]
    src = jnp.asarray([r // SUBLANES for r in _wprep_source_rows()], jnp.int32)
    grid_spec = pltpu.PrefetchScalarGridSpec(
        num_scalar_prefetch=1,
        grid=(N_MAIN // WPREP_ROWS,),
        in_specs=[
            pl.BlockSpec((pl.Element(WPREP_ROWS), pl.Element(d)), lambda i, src: (src[i] * SUBLANES, 0)),
        ],
        out_specs=pl.BlockSpec((WPREP_ROWS, d), lambda i, src: (i, 0)),
    )
    return pl.pallas_call(
        _wprep_kernel,
        grid_spec=grid_spec,
        out_shape=jax.ShapeDtypeStruct((N_MAIN, d), BF16),
        compiler_params=_params("arbitrary"),
        name="w_in_layout",
    )(src, w_t)


_NT = (((1,), (1,)), ((), ()))


def _in_proj_kernel(x_ref, g_ref, w_ref, wf_ref, cs_ref, o_ref, flog_ref, h_ref, *, row_chunk):
    n = pl.program_id(1)
    tm = x_ref.shape[0]

    @pl.when(n == 0)
    def _():
        def chunk(ci, carry):
            r = pl.multiple_of(ci * row_chunk, row_chunk)
            x = x_ref[pl.ds(r, row_chunk), :]
            ms = jnp.mean(x * x, axis=-1, keepdims=True)
            h = x * lax.rsqrt(ms + EPS) * g_ref[...]
            h_ref[pl.ds(r, row_chunk), :] = h.astype(BF16)
            return carry

        lax.fori_loop(0, tm // row_chunk, chunk, 0)
        row = lax.broadcasted_iota(jnp.int32, wf_ref.shape, 0)
        wf = jnp.where(row < N_FOX_HEADS, wf_ref[...], 0.0).astype(BF16)
        flog_ref[...] = lax.dot_general(h_ref[...], wf, _NT, preferred_element_type=F32)

    acc = lax.dot_general(h_ref[...], w_ref[...], _NT, preferred_element_type=F32)
    o_ref[...] = (acc * cs_ref[...]).astype(BF16)


def _in_proj(x, norm_g, w_main, w_t, col_scale, *, tm, tn):
    s, d = x.shape
    n_main = w_main.shape[0]
    return pl.pallas_call(
        functools.partial(_in_proj_kernel, row_chunk=128),
        grid=(s // tm, n_main // tn),
        in_specs=[
            pl.BlockSpec((tm, d), lambda i, n: (i, 0)),
            pl.BlockSpec((1, d), lambda i, n: (0, 0)),
            pl.BlockSpec((tn, d), lambda i, n: (n, 0)),
            pl.BlockSpec((pl.Element(LANES), pl.Element(d)), lambda i, n: (REF_F_LOGIT, 0)),
            pl.BlockSpec((1, tn), lambda i, n: (0, n)),
        ],
        out_specs=[
            pl.BlockSpec((tm, tn), lambda i, n: (i, n)),
            pl.BlockSpec((tm, LANES), lambda i, n: (i, 0)),
        ],
        out_shape=[
            jax.ShapeDtypeStruct((s, n_main), BF16),
            jax.ShapeDtypeStruct((s, LANES), F32),
        ],
        scratch_shapes=[pltpu.VMEM((tm, d), BF16)],
        compiler_params=_params("arbitrary", "arbitrary"),
        name="in_proj",
    )(x, norm_g, w_main, w_t, col_scale)


def _decay_kernel(flog_ref, bf_ref, c_hs_ref, c_sh_ref):
    s = flog_ref.shape[0]
    logf = jax.nn.log_sigmoid(flog_ref[...] + bf_ref[...])
    c = logf.T[:N_FOX_HEADS, :]
    pos = lax.broadcasted_iota(jnp.int32, c.shape, 1)
    shift = 1
    while shift < s:
        c = c + jnp.where(pos >= shift, pltpu.roll(c, shift, 1), 0.0)
        shift *= 2
    c = c * LOG2E
    c_hs_ref[...] = c
    c_sh_ref[...] = jnp.concatenate([c, jnp.zeros((LANES - N_FOX_HEADS, s), F32)], axis=0).T


def _decay(flog, b_f_row):
    s = flog.shape[0]
    return pl.pallas_call(
        _decay_kernel,
        out_shape=[
            jax.ShapeDtypeStruct((N_FOX_HEADS, s), F32),
            jax.ShapeDtypeStruct((s, LANES), F32),
        ],
        compiler_params=pltpu.CompilerParams(vmem_limit_bytes=VMEM_LIMIT_BYTES),
        name="decay_cumsum",
    )(flog, b_f_row)


def _conv_kernel(a_ref, b_ref, ap_ref, bp_ref, gate_ref, cw_ref, cb_ref, lg_ref, lb_ref, pw_ref,
                 o_ref, u_ref, c_ref, *, row_chunk):
    i = pl.program_id(0)
    tm = a_ref.shape[0]

    shifted_rows = tm + CONV_HALO - SUBLANES
    prev = jnp.where(i > 0, ap_ref[...].astype(F32) * jax.nn.sigmoid(bp_ref[...].astype(F32)), 0.0)
    cur = a_ref[...].astype(F32) * jax.nn.sigmoid(b_ref[...].astype(F32))
    for g in range(CONV_W // LANES):
        lo = g * LANES
        u_ref[0, g, 0:CONV_HALO, :] = prev[:, lo:lo + LANES]
        u_ref[0, g, CONV_HALO:CONV_HALO + tm, :] = cur[:, lo:lo + LANES]
        for p in range(1, SUBLANES):
            u_ref[p, g, 0:shifted_rows, :] = u_ref[0, g, p:p + shifted_rows, :]

    first_tap = CONV_HALO - (CONV_K - 1)

    for g in range(CONV_W // LANES):
        lo = g * LANES
        taps_w = [jnp.broadcast_to(cw_ref[k:k + 1, lo:lo + LANES], (SUBLANES, LANES)) for k in range(CONV_K)]
        bias = cb_ref[:, lo:lo + LANES]

        def chunk(ci, carry, g=g, lo=lo, taps_w=taps_w, bias=bias):
            r = pl.multiple_of(ci * row_chunk, row_chunk)
            acc = jnp.zeros((row_chunk, LANES), F32) + bias
            for k in range(CONV_K):
                phase, base = (first_tap + k) % SUBLANES, (first_tap + k) // SUBLANES * SUBLANES
                tap = u_ref[phase, g, pl.ds(pl.multiple_of(r + base, SUBLANES), row_chunk), :]
                w = jnp.concatenate([taps_w[k]] * (row_chunk // SUBLANES), axis=0)
                acc = acc + w * tap
            c_ref[pl.ds(r, row_chunk), lo:lo + LANES] = acc
            return carry

        lax.fori_loop(0, tm // row_chunk, chunk, 0)

    conv = c_ref[...]
    mu = jnp.mean(conv, axis=-1, keepdims=True)
    cen = conv - mu
    var = jnp.mean(cen * cen, axis=-1, keepdims=True)
    y = cen * lax.rsqrt(var + EPS) * lg_ref[...] + lb_ref[...]
    y = _silu(y).astype(BF16)
    pw = jnp.dot(y, pw_ref[...], preferred_element_type=F32)
    o_ref[...] = (pw * _silu(gate_ref[...].astype(F32))).astype(BF16)


def _conv_branch(proj, conv_w, conv_b, ln_g, ln_b, w_pw, *, tm):
    s = proj.shape[0]
    halo_blocks = tm // CONV_HALO
    col = lambda c: c // CONV_W

    def prev_map(c):
        return lambda i: (jnp.maximum(i * halo_blocks - 1, 0), col(c))

    full = lambda shape: pl.BlockSpec(shape, lambda i: (0, 0))
    return pl.pallas_call(
        functools.partial(_conv_kernel, row_chunk=128),
        grid=(s // tm,),
        in_specs=[
            pl.BlockSpec((tm, CONV_W), lambda i: (i, col(COL_CONV_A))),
            pl.BlockSpec((tm, CONV_W), lambda i: (i, col(COL_CONV_B))),
            pl.BlockSpec((CONV_HALO, CONV_W), prev_map(COL_CONV_A)),
            pl.BlockSpec((CONV_HALO, CONV_W), prev_map(COL_CONV_B)),
            pl.BlockSpec((tm, CONV_W), lambda i: (i, col(COL_CONV_GATE))),
            full((CONV_K, CONV_W)),
            full((1, CONV_W)),
            full((1, CONV_W)),
            full((1, CONV_W)),
            full((CONV_W, CONV_W)),
        ],
        out_specs=pl.BlockSpec((tm, CONV_W), lambda i: (i, 0)),
        out_shape=jax.ShapeDtypeStruct((s, CONV_W), BF16),
        scratch_shapes=[pltpu.VMEM((SUBLANES, CONV_W // LANES, CONV_HALO + tm, LANES), F32),
                        pltpu.VMEM((tm, CONV_W), F32)],
        compiler_params=_params("arbitrary"),
        name="conv_branch",
    )(proj, proj, proj, proj, proj, conv_w, conv_b, ln_g, ln_b, w_pw)


def _fox_kernel(noff_ref, q_ref, k_ref, v_ref, crow_ref, gate_ref, o_ref, m_ref, l_ref, acc_ref, *, heads, tk):
    hp = pl.program_id(0)
    i = pl.program_id(1)
    tq = q_ref.shape[0]
    lane_reps = tk // LANES

    q_start = pl.multiple_of(i * tq, tq)
    cbase = [crow_ref[hp * heads + g, :, pl.ds(q_start, LANES)][:, 0:1] for g in range(heads)]

    m_ref[...] = jnp.full(m_ref.shape, MASK_VALUE, F32)
    l_ref[...] = jnp.zeros(l_ref.shape, F32)
    acc_ref[...] = jnp.zeros(acc_ref.shape, F32)

    def key_start(jj):
        return pl.multiple_of(jnp.maximum(q_start + tq - (jj + 1) * tk, 0), tq)

    def step(jj, masked, which=tuple(range(heads))):
        start = key_start(jj)
        if masked:
            limit = jnp.where(jj == 0, q_start + tq, q_start + tq - jj * tk)
            qpos = q_start + lax.broadcasted_iota(jnp.int32, (tq, tk), 0)
            kpos = start + lax.broadcasted_iota(jnp.int32, (tq, tk), 1)
            visible = (kpos <= qpos) & (kpos < limit)

        def logits(g):
            lo = g * HEAD_DIM
            q = q_ref[:, lo:lo + HEAD_DIM]
            k = k_ref[pl.ds(start, tk), lo:lo + HEAD_DIM]
            ck = crow_ref[hp * heads + g, :, pl.ds(start, tk)] - cbase[g]
            s = lax.dot_general(q, k, (((1,), (1,)), ((), ())), preferred_element_type=F32) - ck
            return jnp.where(visible, s, MASK_VALUE) if masked else s

        s_next = logits(which[0])
        for pos, g in enumerate(which):
            s = s_next
            if pos + 1 < len(which):
                s_next = logits(which[pos + 1])
            lo = g * HEAD_DIM
            v = v_ref[pl.ds(start, tk), lo:lo + HEAD_DIM]
            m_old = m_ref[g]
            m_new = jnp.maximum(m_old, jnp.max(s, axis=1, keepdims=True))
            alpha = jnp.exp2(m_old - m_new)
            p = jnp.exp2(s - jnp.concatenate([m_new] * lane_reps, axis=1))
            l_ref[g] = alpha * l_ref[g] + jnp.sum(p, axis=1, keepdims=True)
            acc_ref[g] = alpha * acc_ref[g] + jnp.dot(p.astype(BF16), v, preferred_element_type=F32)
            m_ref[g] = m_new

    n_off = [noff_ref[hp * heads + g, i] for g in range(heads)]
    n_common = functools.reduce(jnp.minimum, n_off)

    step(0, True)

    def body(jj, carry):
        step(jj, False)
        return carry

    lax.fori_loop(1, n_common, body, 0)

    common_is_clamped = q_start + tq - (n_common + 1) * tk < 0

    @pl.when((n_common > 0) & common_is_clamped)
    def _():
        step(n_common, True)

    @pl.when((n_common > 0) & jnp.logical_not(common_is_clamped))
    def _():
        step(n_common, False)

    for g in range(heads):
        def tail(jj, carry, g=g):
            step(jj, True, which=(g,))
            return carry

        lax.fori_loop(n_common + 1, n_off[g] + 1, tail, 0)

    for g in range(heads):
        lo = g * HEAD_DIM
        gate = gate_ref[:, lo:lo + HEAD_DIM].astype(F32)
        o_ref[:, lo:lo + HEAD_DIM] = (acc_ref[g] / l_ref[g] * _silu(gate)).astype(BF16)


def _fox_attention(n_off, proj, c_hs, *, tq, tk, heads):
    s = proj.shape[0]
    w = heads * HEAD_DIM
    gcol = lambda c: c // w
    grid_spec = pltpu.PrefetchScalarGridSpec(
        num_scalar_prefetch=1,
        grid=(N_FOX_HEADS // heads, s // tq),
        in_specs=[
            pl.BlockSpec((tq, w), lambda h, i, n: (i, gcol(COL_Q) + h)),
            pl.BlockSpec((s, w), lambda h, i, n: (0, gcol(COL_K) + h)),
            pl.BlockSpec((s, w), lambda h, i, n: (0, gcol(COL_V) + h)),
            pl.BlockSpec((N_FOX_HEADS, 1, s), lambda h, i, n: (0, 0, 0)),
            pl.BlockSpec((tq, w), lambda h, i, n: (i, gcol(COL_FOX_GATE) + h)),
        ],
        out_specs=pl.BlockSpec((tq, w), lambda h, i, n: (i, h)),
        scratch_shapes=[
            pltpu.VMEM((heads, tq, LANES), F32),
            pltpu.VMEM((heads, tq, LANES), F32),
            pltpu.VMEM((heads, tq, HEAD_DIM), F32),
        ],
    )
    return pl.pallas_call(
        functools.partial(_fox_kernel, heads=heads, tk=tk),
        grid_spec=grid_spec,
        out_shape=jax.ShapeDtypeStruct((s, FOX_W), BF16),
        compiler_params=_params("arbitrary", "arbitrary"),
        name="fox_attention",
    )(n_off, proj, proj, proj, c_hs, proj)


def _bounds_kernel(q_ref, k_ref, c_ref, o_ref, e_ref, qmax_ref, kmax_ref, dmin_ref, cfirst_ref, clast_ref, *,
                   tq, tk):
    step = pl.program_id(0)
    tiles_per_step = q_ref.shape[0] // tq
    ratio = tk // tq

    @pl.when(step == 0)
    def _():
        col_head = lax.broadcasted_iota(jnp.int32, e_ref.shape, 0) // HEAD_DIM
        head = lax.broadcasted_iota(jnp.int32, e_ref.shape, 1)
        e_ref[...] = jnp.where(col_head == head, 1.0, 0.0).astype(BF16)

    q = q_ref[...]
    k = k_ref[...]
    e = e_ref[...]
    head_sum = lambda x: jnp.dot(x, e, preferred_element_type=F32)
    qn = jnp.sqrt(head_sum(q * q))
    kn = jnp.sqrt(head_sum(k * k))
    d = head_sum(q * k) - BOUND_SLACK * qn * kn
    rep = lambda x: jnp.broadcast_to(x, (8, LANES))
    for t in range(tiles_per_step):
        i = step * tiles_per_step + t
        rows = slice(t * tq, (t + 1) * tq)
        qmax_ref[i] = rep(jnp.max(qn[rows], axis=0, keepdims=True))
        kmax_ref[i] = rep(jnp.max(kn[rows], axis=0, keepdims=True))
        dmin_ref[i] = rep(jnp.min(d[rows], axis=0, keepdims=True))
        cfirst_ref[i] = rep(c_ref[t * tq:t * tq + 1, :])
        clast_ref[i] = rep(c_ref[(t + 1) * tq - 1:(t + 1) * tq, :])

    @pl.when(step == pl.num_programs(0) - 1)
    def _():
        kmax = jnp.max(kmax_ref[...], axis=0, keepdims=True)
        reach = qmax_ref[...] * kmax * (1.0 + BOUND_SLACK) - dmin_ref[...] + SKIP_MARGIN
        cfirst = cfirst_ref[...]
        tile = lax.broadcasted_iota(jnp.int32, cfirst.shape, 0)
        count = jnp.zeros(cfirst.shape, jnp.int32)
        for j in range(o_ref.shape[0]):
            is_key_tile_end = (tile > j) & ((tile - j) % ratio == 0)
            needed = (clast_ref[j:j + 1] - cfirst <= reach) & is_key_tile_end
            count = count + needed.astype(jnp.int32)
        o_ref[...] = count


def _fox_bounds(proj, c_sh, *, tq, tk, rows):
    s = proj.shape[0]
    nq = s // tq
    stat = pltpu.VMEM((nq, 8, LANES), F32)
    return pl.pallas_call(
        functools.partial(_bounds_kernel, tq=tq, tk=tk),
        grid=(s // rows,),
        in_specs=[
            pl.BlockSpec((rows, FOX_W), lambda i: (i, COL_Q // FOX_W)),
            pl.BlockSpec((rows, FOX_W), lambda i: (i, COL_K // FOX_W)),
            pl.BlockSpec((rows, LANES), lambda i: (i, 0)),
        ],
        out_specs=pl.BlockSpec((nq, 8, LANES), lambda i: (0, 0, 0)),
        out_shape=jax.ShapeDtypeStruct((nq, 8, LANES), jnp.int32),
        scratch_shapes=[pltpu.VMEM((FOX_W, LANES), BF16), stat, stat, stat, stat, stat],
        compiler_params=_params("arbitrary"),
        name="fox_bounds",
    )(proj, proj, c_sh)


def _mem_kernel(mem_ref, mg_ref, wkv_ref, q_ref, gate_ref, o_ref, kv_ref):
    i = pl.program_id(0)

    @pl.when(i == 0)
    def _():
        mem = mem_ref[...]
        ms = jnp.mean(mem * mem, axis=-1, keepdims=True)
        hm = (mem * lax.rsqrt(ms + EPS) * mg_ref[...]).astype(BF16)
        kv_ref[...] = jnp.dot(hm, wkv_ref[...], preferred_element_type=F32).astype(BF16)

    def logits(hh):
        lo = hh * HEAD_DIM
        return lax.dot_general(q_ref[:, lo:lo + HEAD_DIM], kv_ref[:, lo:lo + HEAD_DIM], _NT,
                               preferred_element_type=F32)

    s_next = logits(0)
    for hh in range(N_MEM_HEADS):
        s = s_next
        if hh + 1 < N_MEM_HEADS:
            s_next = logits(hh + 1)
        lo = hh * HEAD_DIM
        v = kv_ref[:, MEM_W + lo:MEM_W + lo + HEAD_DIM]
        p = jnp.exp2(s - jnp.max(s, axis=1, keepdims=True))
        l = jnp.sum(p, axis=1, keepdims=True)
        pv = jnp.dot(p.astype(BF16), v, preferred_element_type=F32)
        gate = gate_ref[:, lo:lo + HEAD_DIM].astype(F32)
        o_ref[:, lo:lo + HEAD_DIM] = (pv / l * _silu(gate)).astype(BF16)


def _mem_attention(proj, mem, mem_norm_g, w_kv, *, tm):
    s = proj.shape[0]
    m_len, d = mem.shape
    col = lambda c: c // MEM_W
    full = lambda shape: pl.BlockSpec(shape, lambda i: (0, 0))
    return pl.pallas_call(
        _mem_kernel,
        grid=(s // tm,),
        in_specs=[
            full((m_len, d)),
            full((1, d)),
            full((d, 2 * MEM_W)),
            pl.BlockSpec((tm, MEM_W), lambda i: (i, col(COL_MQ))),
            pl.BlockSpec((tm, MEM_W), lambda i: (i, col(COL_MEM_GATE))),
        ],
        out_specs=pl.BlockSpec((tm, MEM_W), lambda i: (i, 0)),
        out_shape=jax.ShapeDtypeStruct((s, MEM_W), BF16),
        scratch_shapes=[pltpu.VMEM((m_len, 2 * MEM_W), BF16)],
        compiler_params=_params("arbitrary"),
        name="mem_attention",
    )(mem, mem_norm_g, w_kv, proj, proj)


def _out_kernel(x_ref, yc_ref, yf_ref, ym_ref, w_ref, g_ref, o_ref):
    acc = jnp.dot(yc_ref[...], w_ref[0:CONV_W, :], preferred_element_type=F32)
    acc = acc + jnp.dot(yf_ref[...], w_ref[CONV_W:CONV_W + FOX_W, :], preferred_element_type=F32)
    acc = acc + jnp.dot(ym_ref[...], w_ref[CONV_W + FOX_W:, :], preferred_element_type=F32)
    r = x_ref[...] + acc
    ms = jnp.mean(r * r, axis=-1, keepdims=True)
    o_ref[...] = r * lax.rsqrt(ms + EPS) * g_ref[...]


def _out_proj(x, y_conv, y_fox, y_mem, w_out, final_g, *, tm):
    s, d = x.shape
    row = lambda w: pl.BlockSpec((tm, w), lambda i: (i, 0))
    return pl.pallas_call(
        _out_kernel,
        grid=(s // tm,),
        in_specs=[
            row(d), row(CONV_W), row(FOX_W), row(MEM_W),
            pl.BlockSpec(w_out.shape, lambda i: (0, 0)),
            pl.BlockSpec((1, d), lambda i: (0, 0)),
        ],
        out_specs=row(d),
        out_shape=jax.ShapeDtypeStruct((s, d), F32),
        compiler_params=_params("arbitrary"),
        name="out_proj",
    )(x, y_conv, y_fox, y_mem, w_out, final_g)


def _layer(x, mem, norm_g, mem_norm_g, w_in, b_f, conv_w, conv_b, ln_g, ln_b, w_pw, w_kv, w_out):
    d = x.shape[-1]
    w_t = w_in.T
    w_main = _wprep(w_t)
    qk_scale = HEAD_DIM ** -0.5 * LOG2E
    col_scale = jnp.ones((1, N_MAIN), F32).at[:, COL_Q:COL_K].set(qk_scale).at[:, COL_MQ:COL_MEM_GATE].set(qk_scale)
    b_f_row = jnp.pad(b_f, (0, LANES - N_FOX_HEADS)).reshape(1, LANES)

    proj, flog = _in_proj(x, norm_g.reshape(1, d), w_main, w_t, col_scale, tm=1024, tn=1664)
    c_hs, c_sh = _decay(flog, b_f_row)
    y_conv = _conv_branch(proj, conv_w, conv_b.reshape(1, -1), ln_g.reshape(1, -1), ln_b.reshape(1, -1),
                          w_pw.astype(BF16), tm=512)
    n_off = _fox_bounds(proj, c_sh, tq=FOX_TQ, tk=FOX_TK, rows=1024)[:, 0, :N_FOX_HEADS].T
    y_fox = _fox_attention(n_off, proj, c_hs.reshape(N_FOX_HEADS, 1, -1), tq=FOX_TQ, tk=FOX_TK, heads=4)
    y_mem = _mem_attention(proj, mem, mem_norm_g.reshape(1, d), w_kv.astype(BF16), tm=1024)
    return y_conv, y_fox, y_mem


def kernel(x, mem, norm_g, mem_norm_g, w_in, b_f, conv_w, conv_b, conv_ln_g, conv_ln_b, w_conv_pw,
           w_mem_kv, w_out, final_g):
    b, s, d = x.shape
    assert w_in.shape[0] == 1, "a single trunk layer is supported"
    outs = []
    for bi in range(b):
        y_conv, y_fox, y_mem = _layer(
            x[bi], mem[bi], norm_g[0], mem_norm_g[0], w_in[0], b_f[0], conv_w[0], conv_b[0],
            conv_ln_g[0], conv_ln_b[0], w_conv_pw[0], w_mem_kv[0], w_out[0])
        outs.append(_out_proj(x[bi], y_conv, y_fox, y_mem, w_out[0].astype(BF16),
                              final_g.reshape(1, d), tm=512))
    return jnp.stack(outs, axis=0)
```

```python
import functools

import jax
import jax.numpy as jnp
from jax import lax
from jax.experimental import pallas as pl
from jax.experimental.pallas import tpu as pltpu

F32 = jnp.float32
BF16 = jnp.bfloat16

LANES = 128
SUBLANES = 8
HEAD_DIM = 128
N_FOX_HEADS = 8
N_MEM_HEADS = 4
CONV_W = 512
FOX_W = N_FOX_HEADS * HEAD_DIM
MEM_W = N_MEM_HEADS * HEAD_DIM
CONV_K = 31
CONV_HALO = 32
EPS = 1e-6
MASK_VALUE = -1e30
LOG2E = 1.4426950408889634

COL_Q = 0
COL_K = COL_Q + FOX_W
COL_V = COL_K + FOX_W
COL_FOX_GATE = COL_V + FOX_W
COL_CONV_A = COL_FOX_GATE + FOX_W
COL_CONV_B = COL_CONV_A + CONV_W
COL_CONV_GATE = COL_CONV_B + CONV_W
COL_MQ = COL_CONV_GATE + CONV_W
COL_MEM_GATE = COL_MQ + MEM_W
N_MAIN = COL_MEM_GATE + MEM_W

FOX_TQ = 256
FOX_TK = 512
SKIP_MARGIN = 152.0
BOUND_SLACK = 2.0 ** -7

VMEM_LIMIT_BYTES = 56 * 1024 * 1024


def _params(*sem):
    return pltpu.CompilerParams(dimension_semantics=sem, vmem_limit_bytes=VMEM_LIMIT_BYTES)


def _silu(x):
    return x * jax.nn.sigmoid(x)


REF_CONV_A = 0
REF_Q = 3 * CONV_W
REF_F_LOGIT = REF_Q + 3 * FOX_W
REF_FOX_GATE = REF_F_LOGIT + N_FOX_HEADS
REF_MQ = REF_FOX_GATE + FOX_W
WPREP_ROWS = 512


def _wprep_source_rows():
    segments = [(REF_Q, 3 * FOX_W), (REF_FOX_GATE, FOX_W), (REF_CONV_A, 3 * CONV_W), (REF_MQ, 2 * MEM_W)]
    rows = []
    for start, width in segments:
        rows += [start + r for r in range(0, width, WPREP_ROWS)]
    assert len(rows) * WPREP_ROWS == N_MAIN
    return rows


def _wprep_kernel(src_ref, w_ref, wm_ref):
    del src_ref
    wm_ref[...] = w_ref[...].astype(BF16)


def _wprep(w_t):
    d = w_t.shape[1]
    src = jnp.asarray([r // SUBLANES for r in _wprep_source_rows()], jnp.int32)
    grid_spec = pltpu.PrefetchScalarGridSpec(
        num_scalar_prefetch=1,
        grid=(N_MAIN // WPREP_ROWS,),
        in_specs=[
            pl.BlockSpec((pl.Element(WPREP_ROWS), pl.Element(d)), lambda i, src: (src[i] * SUBLANES, 0)),
        ],
        out_specs=pl.BlockSpec((WPREP_ROWS, d), lambda i, src: (i, 0)),
    )
    return pl.pallas_call(
        _wprep_kernel,
        grid_spec=grid_spec,
        out_shape=jax.ShapeDtypeStruct((N_MAIN, d), BF16),
        compiler_params=_params("arbitrary"),
        name="w_in_layout",
    )(src, w_t)


_NT = (((1,), (1,)), ((), ()))


def _in_proj_kernel(x_ref, g_ref, w_ref, wf_ref, cs_ref, o_ref, flog_ref, h_ref, *, row_chunk):
    n = pl.program_id(1)
    tm = x_ref.shape[0]

    @pl.when(n == 0)
    def _():
        def chunk(ci, carry):
            r = pl.multiple_of(ci * row_chunk, row_chunk)
            x = x_ref[pl.ds(r, row_chunk), :]
            ms = jnp.mean(x * x, axis=-1, keepdims=True)
            h = x * lax.rsqrt(ms + EPS) * g_ref[...]
            h_ref[pl.ds(r, row_chunk), :] = h.astype(BF16)
            return carry

        lax.fori_loop(0, tm // row_chunk, chunk, 0)
        row = lax.broadcasted_iota(jnp.int32, wf_ref.shape, 0)
        wf = jnp.where(row < N_FOX_HEADS, wf_ref[...], 0.0).astype(BF16)
        flog_ref[...] = lax.dot_general(h_ref[...], wf, _NT, preferred_element_type=F32)

    acc = lax.dot_general(h_ref[...], w_ref[...], _NT, preferred_element_type=F32)
    o_ref[...] = (acc * cs_ref[...]).astype(BF16)


def _in_proj(x, norm_g, w_main, w_t, col_scale, *, tm, tn):
    s, d = x.shape
    n_main = w_main.shape[0]
    return pl.pallas_call(
        functools.partial(_in_proj_kernel, row_chunk=128),
        grid=(s // tm, n_main // tn),
        in_specs=[
            pl.BlockSpec((tm, d), lambda i, n: (i, 0)),
            pl.BlockSpec((1, d), lambda i, n: (0, 0)),
            pl.BlockSpec((tn, d), lambda i, n: (n, 0)),
            pl.BlockSpec((pl.Element(LANES), pl.Element(d)), lambda i, n: (REF_F_LOGIT, 0)),
            pl.BlockSpec((1, tn), lambda i, n: (0, n)),
        ],
        out_specs=[
            pl.BlockSpec((tm, tn), lambda i, n: (i, n)),
            pl.BlockSpec((tm, LANES), lambda i, n: (i, 0)),
        ],
        out_shape=[
            jax.ShapeDtypeStruct((s, n_main), BF16),
            jax.ShapeDtypeStruct((s, LANES), F32),
        ],
        scratch_shapes=[pltpu.VMEM((tm, d), BF16)],
        compiler_params=_params("arbitrary", "arbitrary"),
        name="in_proj",
    )(x, norm_g, w_main, w_t, col_scale)


def _decay_kernel(flog_ref, bf_ref, c_hs_ref, c_sh_ref):
    s = flog_ref.shape[0]
    logf = jax.nn.log_sigmoid(flog_ref[...] + bf_ref[...])
    c = logf.T[:N_FOX_HEADS, :]
    pos = lax.broadcasted_iota(jnp.int32, c.shape, 1)
    shift = 1
    while shift < s:
        c = c + jnp.where(pos >= shift, pltpu.roll(c, shift, 1), 0.0)
        shift *= 2
    c = c * LOG2E
    c_hs_ref[...] = c
    c_sh_ref[...] = jnp.concatenate([c, jnp.zeros((LANES - N_FOX_HEADS, s), F32)], axis=0).T


def _decay(flog, b_f_row):
    s = flog.shape[0]
    return pl.pallas_call(
        _decay_kernel,
        out_shape=[
            jax.ShapeDtypeStruct((N_FOX_HEADS, s), F32),
            jax.ShapeDtypeStruct((s, LANES), F32),
        ],
        compiler_params=pltpu.CompilerParams(vmem_limit_bytes=VMEM_LIMIT_BYTES),
        name="decay_cumsum",
    )(flog, b_f_row)


def _conv_kernel(a_ref, b_ref, ap_ref, bp_ref, gate_ref, cw_ref, cb_ref, lg_ref, lb_ref, pw_ref,
                 o_ref, u_ref, c_ref, *, row_chunk):
    i = pl.program_id(0)
    tm = a_ref.shape[0]

    shifted_rows = tm + CONV_HALO - SUBLANES
    prev = jnp.where(i > 0, ap_ref[...].astype(F32) * jax.nn.sigmoid(bp_ref[...].astype(F32)), 0.0)
    cur = a_ref[...].astype(F32) * jax.nn.sigmoid(b_ref[...].astype(F32))
    for g in range(CONV_W // LANES):
        lo = g * LANES
        u_ref[0, g, 0:CONV_HALO, :] = prev[:, lo:lo + LANES]
        u_ref[0, g, CONV_HALO:CONV_HALO + tm, :] = cur[:, lo:lo + LANES]
        for p in range(1, SUBLANES):
            u_ref[p, g, 0:shifted_rows, :] = u_ref[0, g, p:p + shifted_rows, :]

    first_tap = CONV_HALO - (CONV_K - 1)

    for g in range(CONV_W // LANES):
        lo = g * LANES
        taps_w = [jnp.broadcast_to(cw_ref[k:k + 1, lo:lo + LANES], (SUBLANES, LANES)) for k in range(CONV_K)]
        bias = cb_ref[:, lo:lo + LANES]

        def chunk(ci, carry, g=g, lo=lo, taps_w=taps_w, bias=bias):
            r = pl.multiple_of(ci * row_chunk, row_chunk)
            acc = jnp.zeros((row_chunk, LANES), F32) + bias
            for k in range(CONV_K):
                phase, base = (first_tap + k) % SUBLANES, (first_tap + k) // SUBLANES * SUBLANES
                tap = u_ref[phase, g, pl.ds(pl.multiple_of(r + base, SUBLANES), row_chunk), :]
                w = jnp.concatenate([taps_w[k]] * (row_chunk // SUBLANES), axis=0)
                acc = acc + w * tap
            c_ref[pl.ds(r, row_chunk), lo:lo + LANES] = acc
            return carry

        lax.fori_loop(0, tm // row_chunk, chunk, 0)

    conv = c_ref[...]
    mu = jnp.mean(conv, axis=-1, keepdims=True)
    cen = conv - mu
    var = jnp.mean(cen * cen, axis=-1, keepdims=True)
    y = cen * lax.rsqrt(var + EPS) * lg_ref[...] + lb_ref[...]
    y = _silu(y).astype(BF16)
    pw = jnp.dot(y, pw_ref[...].astype(BF16), preferred_element_type=F32)
    o_ref[...] = (pw * _silu(gate_ref[...].astype(F32))).astype(BF16)


def _conv_branch(proj, conv_w, conv_b, ln_g, ln_b, w_pw, *, tm):
    s = proj.shape[0]
    halo_blocks = tm // CONV_HALO
    col = lambda c: c // CONV_W

    def prev_map(c):
        return lambda i: (jnp.maximum(i * halo_blocks - 1, 0), col(c))

    full = lambda shape: pl.BlockSpec(shape, lambda i: (0, 0))
    return pl.pallas_call(
        functools.partial(_conv_kernel, row_chunk=128),
        grid=(s // tm,),
        in_specs=[
            pl.BlockSpec((tm, CONV_W), lambda i: (i, col(COL_CONV_A))),
            pl.BlockSpec((tm, CONV_W), lambda i: (i, col(COL_CONV_B))),
            pl.BlockSpec((CONV_HALO, CONV_W), prev_map(COL_CONV_A)),
            pl.BlockSpec((CONV_HALO, CONV_W), prev_map(COL_CONV_B)),
            pl.BlockSpec((tm, CONV_W), lambda i: (i, col(COL_CONV_GATE))),
            full((CONV_K, CONV_W)),
            full((1, CONV_W)),
            full((1, CONV_W)),
            full((1, CONV_W)),
            full((CONV_W, CONV_W)),
        ],
        out_specs=pl.BlockSpec((tm, CONV_W), lambda i: (i, 0)),
        out_shape=jax.ShapeDtypeStruct((s, CONV_W), BF16),
        scratch_shapes=[pltpu.VMEM((SUBLANES, CONV_W // LANES, CONV_HALO + tm, LANES), F32),
                        pltpu.VMEM((tm, CONV_W), F32)],
        compiler_params=_params("arbitrary"),
        name="conv_branch",
    )(proj, proj, proj, proj, proj, conv_w, conv_b, ln_g, ln_b, w_pw)


def _fox_kernel(noff_ref, q_ref, k_ref, v_ref, crow_ref, gate_ref, o_ref, m_ref, l_ref, acc_ref, *, heads, tq, tk):
    tiles_per_step = q_ref.shape[0] // tq

    def tile(sub, carry):
        _fox_tile(pl.program_id(1) * tiles_per_step + sub, pl.multiple_of(sub * tq, tq), noff_ref, q_ref, k_ref,
                  v_ref, crow_ref, gate_ref, o_ref, m_ref, l_ref, acc_ref, heads=heads, tq=tq, tk=tk)
        return carry

    lax.fori_loop(0, tiles_per_step, tile, 0)


def _fox_tile(i, row0, noff_ref, q_ref, k_ref, v_ref, crow_ref, gate_ref, o_ref, m_ref, l_ref, acc_ref, *,
              heads, tq, tk):
    hp = pl.program_id(0)
    lane_reps = tk // LANES

    q_start = pl.multiple_of(i * tq, tq)
    cbase = [crow_ref[hp * heads + g, :, pl.ds(q_start, LANES)][:, 0:1] for g in range(heads)]

    m_ref[...] = jnp.full(m_ref.shape, MASK_VALUE, F32)
    l_ref[...] = jnp.zeros(l_ref.shape, F32)
    acc_ref[...] = jnp.zeros(acc_ref.shape, F32)

    def key_start(jj):
        return pl.multiple_of(jnp.maximum(q_start + tq - (jj + 1) * tk, 0), tq)

    def step(jj, masked, which=tuple(range(heads))):
        start = key_start(jj)
        if masked:
            limit = jnp.where(jj == 0, q_start + tq, q_start + tq - jj * tk)
            qpos = q_start + lax.broadcasted_iota(jnp.int32, (tq, tk), 0)
            kpos = start + lax.broadcasted_iota(jnp.int32, (tq, tk), 1)
            visible = (kpos <= qpos) & (kpos < limit)

        def logits(g):
            lo = g * HEAD_DIM
            q = q_ref[pl.ds(row0, tq), lo:lo + HEAD_DIM]
            k = k_ref[pl.ds(start, tk), lo:lo + HEAD_DIM]
            ck = crow_ref[hp * heads + g, :, pl.ds(start, tk)] - cbase[g]
            s = lax.dot_general(q, k, (((1,), (1,)), ((), ())), preferred_element_type=F32) - ck
            return jnp.where(visible, s, MASK_VALUE) if masked else s

        s_next = logits(which[0])
        for pos, g in enumerate(which):
            s = s_next
            if pos + 1 < len(which):
                s_next = logits(which[pos + 1])
            lo = g * HEAD_DIM
            v = v_ref[pl.ds(start, tk), lo:lo + HEAD_DIM]
            m_old = m_ref[g]
            m_new = jnp.maximum(m_old, jnp.max(s, axis=1, keepdims=True))
            alpha = jnp.exp2(m_old - m_new)
            p = jnp.exp2(s - jnp.concatenate([m_new] * lane_reps, axis=1))
            l_ref[g] = alpha * l_ref[g] + jnp.sum(p, axis=1, keepdims=True)
            acc_ref[g] = alpha * acc_ref[g] + jnp.dot(p.astype(BF16), v, preferred_element_type=F32)
            m_ref[g] = m_new

    n_off = [noff_ref[hp * heads + g, i] for g in range(heads)]
    n_common = functools.reduce(jnp.minimum, n_off)

    step(0, True)

    def body(jj, carry):
        step(jj, False)
        return carry

    lax.fori_loop(1, n_common, body, 0)

    common_is_clamped = q_start + tq - (n_common + 1) * tk < 0

    @pl.when((n_common > 0) & common_is_clamped)
    def _():
        step(n_common, True)

    @pl.when((n_common > 0) & jnp.logical_not(common_is_clamped))
    def _():
        step(n_common, False)

    for g in range(heads):
        def tail(jj, carry, g=g):
            step(jj, True, which=(g,))
            return carry

        lax.fori_loop(n_common + 1, n_off[g] + 1, tail, 0)

    for g in range(heads):
        lo = g * HEAD_DIM
        gate = gate_ref[pl.ds(row0, tq), lo:lo + HEAD_DIM].astype(F32)
        o_ref[pl.ds(row0, tq), lo:lo + HEAD_DIM] = (acc_ref[g] / l_ref[g] * _silu(gate)).astype(BF16)


def _fox_attention(n_off, proj, c_hs, *, tq, tk, heads, rows):
    s = proj.shape[0]
    w = heads * HEAD_DIM
    gcol = lambda c: c // w
    grid_spec = pltpu.PrefetchScalarGridSpec(
        num_scalar_prefetch=1,
        grid=(N_FOX_HEADS // heads, s // rows),
        in_specs=[
            pl.BlockSpec((rows, w), lambda h, i, n: (i, gcol(COL_Q) + h)),
            pl.BlockSpec((s, w), lambda h, i, n: (0, gcol(COL_K) + h)),
            pl.BlockSpec((s, w), lambda h, i, n: (0, gcol(COL_V) + h)),
            pl.BlockSpec((N_FOX_HEADS, 1, s), lambda h, i, n: (0, 0, 0)),
            pl.BlockSpec((rows, w), lambda h, i, n: (i, gcol(COL_FOX_GATE) + h)),
        ],
        out_specs=pl.BlockSpec((rows, w), lambda h, i, n: (i, h)),
        scratch_shapes=[
            pltpu.VMEM((heads, tq, LANES), F32),
            pltpu.VMEM((heads, tq, LANES), F32),
            pltpu.VMEM((heads, tq, HEAD_DIM), F32),
        ],
    )
    return pl.pallas_call(
        functools.partial(_fox_kernel, heads=heads, tq=tq, tk=tk),
        grid_spec=grid_spec,
        out_shape=jax.ShapeDtypeStruct((s, FOX_W), BF16),
        compiler_params=_params("arbitrary", "arbitrary"),
        name="fox_attention",
    )(n_off, proj, proj, proj, c_hs, proj)


def _bounds_kernel(q_ref, k_ref, c_ref, o_ref, e_ref, qmax_ref, kmax_ref, dmin_ref, cfirst_ref, clast_ref, *,
                   tq, tk):
    step = pl.program_id(0)
    tiles_per_step = q_ref.shape[0] // tq
    ratio = tk // tq

    @pl.when(step == 0)
    def _():
        col_head = lax.broadcasted_iota(jnp.int32, e_ref.shape, 0) // HEAD_DIM
        head = lax.broadcasted_iota(jnp.int32, e_ref.shape, 1)
        e_ref[...] = jnp.where(col_head == head, 1.0, 0.0).astype(BF16)

    q = q_ref[...]
    k = k_ref[...]
    e = e_ref[...]
    head_sum = lambda x: jnp.dot(x, e, preferred_element_type=F32)
    qn = jnp.sqrt(head_sum(q * q))
    kn = jnp.sqrt(head_sum(k * k))
    d = head_sum(q * k) - BOUND_SLACK * qn * kn
    rep = lambda x: jnp.broadcast_to(x, (8, LANES))
    for t in range(tiles_per_step):
        i = step * tiles_per_step + t
        rows = slice(t * tq, (t + 1) * tq)
        qmax_ref[i] = rep(jnp.max(qn[rows], axis=0, keepdims=True))
        kmax_ref[i] = rep(jnp.max(kn[rows], axis=0, keepdims=True))
        dmin_ref[i] = rep(jnp.min(d[rows], axis=0, keepdims=True))
        cfirst_ref[i] = rep(c_ref[t * tq:t * tq + 1, :])
        clast_ref[i] = rep(c_ref[(t + 1) * tq - 1:(t + 1) * tq, :])

    @pl.when(step == pl.num_programs(0) - 1)
    def _():
        kmax = jnp.max(kmax_ref[...], axis=0, keepdims=True)
        reach = qmax_ref[...] * kmax * (1.0 + BOUND_SLACK) - dmin_ref[...] + SKIP_MARGIN
        cfirst = cfirst_ref[...]
        tile = lax.broadcasted_iota(jnp.int32, cfirst.shape, 0)
        count = jnp.zeros(cfirst.shape, jnp.int32)
        for j in range(o_ref.shape[0]):
            is_key_tile_end = (tile > j) & ((tile - j) % ratio == 0)
            needed = (clast_ref[j:j + 1] - cfirst <= reach) & is_key_tile_end
            count = count + needed.astype(jnp.int32)
        o_ref[...] = count


def _fox_bounds(proj, c_sh, *, tq, tk, rows):
    s = proj.shape[0]
    nq = s // tq
    stat = pltpu.VMEM((nq, 8, LANES), F32)
    return pl.pallas_call(
        functools.partial(_bounds_kernel, tq=tq, tk=tk),
        grid=(s // rows,),
        in_specs=[
            pl.BlockSpec((rows, FOX_W), lambda i: (i, COL_Q // FOX_W)),
            pl.BlockSpec((rows, FOX_W), lambda i: (i, COL_K // FOX_W)),
            pl.BlockSpec((rows, LANES), lambda i: (i, 0)),
        ],
        out_specs=pl.BlockSpec((nq, 8, LANES), lambda i: (0, 0, 0)),
        out_shape=jax.ShapeDtypeStruct((nq, 8, LANES), jnp.int32),
        scratch_shapes=[pltpu.VMEM((FOX_W, LANES), BF16), stat, stat, stat, stat, stat],
        compiler_params=_params("arbitrary"),
        name="fox_bounds",
    )(proj, proj, c_sh)


def _mem_kernel(mem_ref, mg_ref, wkv_ref, q_ref, gate_ref, o_ref, kv_ref):
    i = pl.program_id(0)

    @pl.when(i == 0)
    def _():
        mem = mem_ref[...]
        ms = jnp.mean(mem * mem, axis=-1, keepdims=True)
        hm = (mem * lax.rsqrt(ms + EPS) * mg_ref[...]).astype(BF16)
        kv_ref[...] = jnp.dot(hm, wkv_ref[...].astype(BF16), preferred_element_type=F32).astype(BF16)

    def logits(hh):
        lo = hh * HEAD_DIM
        return lax.dot_general(q_ref[:, lo:lo + HEAD_DIM], kv_ref[:, lo:lo + HEAD_DIM], _NT,
                               preferred_element_type=F32)

    s_next = logits(0)
    for hh in range(N_MEM_HEADS):
        s = s_next
        if hh + 1 < N_MEM_HEADS:
            s_next = logits(hh + 1)
        lo = hh * HEAD_DIM
        v = kv_ref[:, MEM_W + lo:MEM_W + lo + HEAD_DIM]
        p = jnp.exp2(s - jnp.max(s, axis=1, keepdims=True))
        l = jnp.sum(p, axis=1, keepdims=True)
        pv = jnp.dot(p.astype(BF16), v, preferred_element_type=F32)
        gate = gate_ref[:, lo:lo + HEAD_DIM].astype(F32)
        o_ref[:, lo:lo + HEAD_DIM] = (pv / l * _silu(gate)).astype(BF16)


def _mem_attention(proj, mem, mem_norm_g, w_kv, *, tm):
    s = proj.shape[0]
    m_len, d = mem.shape
    col = lambda c: c // MEM_W
    full = lambda shape: pl.BlockSpec(shape, lambda i: (0, 0))
    return pl.pallas_call(
        _mem_kernel,
        grid=(s // tm,),
        in_specs=[
            full((m_len, d)),
            full((1, d)),
            pl.BlockSpec((d, 2 * MEM_W), lambda i: (0, 0), pipeline_mode=pl.Buffered(1)),
            pl.BlockSpec((tm, MEM_W), lambda i: (i, col(COL_MQ))),
            pl.BlockSpec((tm, MEM_W), lambda i: (i, col(COL_MEM_GATE))),
        ],
        out_specs=pl.BlockSpec((tm, MEM_W), lambda i: (i, 0)),
        out_shape=jax.ShapeDtypeStruct((s, MEM_W), BF16),
        scratch_shapes=[pltpu.VMEM((m_len, 2 * MEM_W), BF16)],
        compiler_params=_params("arbitrary"),
        name="mem_attention",
    )(mem, mem_norm_g, w_kv, proj, proj)


def _out_kernel(x_ref, yc_ref, yf_ref, ym_ref, w32_ref, g_ref, o_ref, w_ref):
    @pl.when(pl.program_id(0) == 0)
    def _():
        w_ref[...] = w32_ref[...].astype(BF16)

    acc = jnp.dot(yc_ref[...], w_ref[0:CONV_W, :], preferred_element_type=F32)
    acc = acc + jnp.dot(yf_ref[...], w_ref[CONV_W:CONV_W + FOX_W, :], preferred_element_type=F32)
    acc = acc + jnp.dot(ym_ref[...], w_ref[CONV_W + FOX_W:, :], preferred_element_type=F32)
    r = x_ref[...] + acc
    ms = jnp.mean(r * r, axis=-1, keepdims=True)
    o_ref[...] = r * lax.rsqrt(ms + EPS) * g_ref[...]


def _out_proj(x, y_conv, y_fox, y_mem, w_out, final_g, *, tm):
    s, d = x.shape
    row = lambda w: pl.BlockSpec((tm, w), lambda i: (i, 0))
    return pl.pallas_call(
        _out_kernel,
        grid=(s // tm,),
        in_specs=[
            row(d), row(CONV_W), row(FOX_W), row(MEM_W),
            pl.BlockSpec(w_out.shape, lambda i: (0, 0), pipeline_mode=pl.Buffered(1)),
            pl.BlockSpec((1, d), lambda i: (0, 0)),
        ],
        out_specs=row(d),
        out_shape=jax.ShapeDtypeStruct((s, d), F32),
        scratch_shapes=[pltpu.VMEM(w_out.shape, BF16)],
        compiler_params=_params("arbitrary"),
        name="out_proj",
    )(x, y_conv, y_fox, y_mem, w_out, final_g)


def _layer(x, mem, norm_g, mem_norm_g, w_in, b_f, conv_w, conv_b, ln_g, ln_b, w_pw, w_kv, w_out):
    d = x.shape[-1]
    w_t = w_in.T
    w_main = _wprep(w_t)
    qk_scale = HEAD_DIM ** -0.5 * LOG2E
    col_scale = jnp.ones((1, N_MAIN), F32).at[:, COL_Q:COL_K].set(qk_scale).at[:, COL_MQ:COL_MEM_GATE].set(qk_scale)
    b_f_row = jnp.pad(b_f, (0, LANES - N_FOX_HEADS)).reshape(1, LANES)

    proj, flog = _in_proj(x, norm_g.reshape(1, d), w_main, w_t, col_scale, tm=1024, tn=1664)
    c_hs, c_sh = _decay(flog, b_f_row)
    y_conv = _conv_branch(proj, conv_w, conv_b.reshape(1, -1), ln_g.reshape(1, -1), ln_b.reshape(1, -1),
                          w_pw, tm=512)
    n_off = _fox_bounds(proj, c_sh, tq=FOX_TQ, tk=FOX_TK, rows=1024)[:, 0, :N_FOX_HEADS].T
    y_fox = _fox_attention(n_off, proj, c_hs.reshape(N_FOX_HEADS, 1, -1), tq=FOX_TQ, tk=FOX_TK, heads=4, rows=1024)
    y_mem = _mem_attention(proj, mem, mem_norm_g.reshape(1, d), w_kv, tm=1024)
    return y_conv, y_fox, y_mem


def kernel(x, mem, norm_g, mem_norm_g, w_in, b_f, conv_w, conv_b, conv_ln_g, conv_ln_b, w_conv_pw,
           w_mem_kv, w_out, final_g):
    b, s, d = x.shape
    assert w_in.shape[0] == 1, "a single trunk layer is supported"
    outs = []
    for bi in range(b):
        y_conv, y_fox, y_mem = _layer(
            x[bi], mem[bi], norm_g[0], mem_norm_g[0], w_in[0], b_f[0], conv_w[0], conv_b[0],
            conv_ln_g[0], conv_ln_b[0], w_conv_pw[0], w_mem_kv[0], w_out[0])
        outs.append(_out_proj(x[bi], y_conv, y_fox, y_mem, w_out[0],
                              final_g.reshape(1, d), tm=512))
    return jnp.stack(outs, axis=0)
```
